```python
import jax, jax.numpy as jnp
from jax import lax
import numpy as np

D_MODEL = 2048
BATCH = 16
SEQ = 256
DEPTH = 2
DEC_BATCH = 2
DEC_SEQ = 1024
PAST_LEN = 512

GRID_W = 64
HEAD_DIM = 128
W_A = D_MODEL // 4
W_B = D_MODEL // 4
W_C = D_MODEL // 4
W_D = D_MODEL // 4
N_HEADS_A = W_A // HEAD_DIM
N_HEADS_B = W_B // HEAD_DIM
N_GROUPS_D = 4
GROUP_D = W_D // N_GROUPS_D
POOL_WINDOWS = (2, 4, 8, 16)
CHUNK_A = 64
CHUNK_B = 128
CONV_W = 3
D_FF = 5632
EPS = 1e-6
SPLIT_SIZES = (W_A,) * 5 + (W_B,) * 2 + (W_C,) * 3 + (W_D,)
D_IN = sum(SPLIT_SIZES)
SPLIT_POINTS = [int(v) for v in np.cumsum(SPLIT_SIZES)[:-1]]

kernel_name = 'hymba_style_diffusion_hgrn2_sgu_conv_pool_step'


def rmsnorm(x, g):
    x32 = x.astype(jnp.float32)
    y = x32 * lax.rsqrt(jnp.mean(x32 * x32, axis=-1, keepdims=True) + EPS)
    return (y * g.astype(jnp.float32)).astype(x.dtype)


def adaln(cvec, w_ada, b_ada):
    m = (jax.nn.silu(cvec) @ w_ada + b_ada)[:, None, :]
    return jnp.split(m, 6, axis=-1)


def dwconv3(x, w):
    return lax.conv_general_dilated(
        x, w[:, None, :].astype(x.dtype), window_strides=(1,),
        padding=[(CONV_W // 2, CONV_W // 2)],
        dimension_numbers=('NWC', 'WIO', 'NWC'), feature_group_count=x.shape[-1])


def grid_pos_embed(n_tokens, dim, dtype):
    rows = n_tokens // GRID_W
    r = jnp.repeat(jnp.arange(rows), GRID_W).astype(jnp.float32)[:, None]
    col = jnp.tile(jnp.arange(GRID_W), rows).astype(jnp.float32)[:, None]
    quarter = dim // 4
    freq = jnp.exp(-jnp.log(10000.0) * jnp.arange(quarter, dtype=jnp.float32) / quarter)[None, :]
    emb = jnp.concatenate([jnp.sin(r * freq), jnp.cos(r * freq), jnp.sin(col * freq), jnp.cos(col * freq)], -1)
    return emb.astype(dtype)


def lower_bounds(lb_logits):
    p = jax.nn.softmax(lb_logits.astype(jnp.float32), axis=0)
    return jnp.cumsum(p, axis=0) - p[0:1]


def hgrn_chunk_scan(q, k, v, logf, s0):
    b, h, t, dk = q.shape
    n = t // CHUNK_A
    def to_chunks(a):
        return a.reshape(b, h, n, CHUNK_A, a.shape[-1]).transpose(2, 0, 1, 3, 4)
    mask = jnp.tril(jnp.ones((CHUNK_A, CHUNK_A), dtype=bool))[:, :, None]
    def step(s, inp):
        qc, kc, vc, lc = inp
        cum = jnp.cumsum(lc, axis=2)
        o_inter = jnp.einsum('bhtk,bhkv->bhtv', qc * jnp.exp(cum), s)
        diff = cum[:, :, :, None, :] - cum[:, :, None, :, :]
        decay = jnp.exp(jnp.where(mask, diff, -jnp.inf))
        scores = jnp.einsum('bhtk,bhsk,bhtsk->bhts', qc, kc, decay)
        o = o_inter + jnp.einsum('bhts,bhsv->bhtv', scores, vc)
        last = cum[:, :, -1:, :]
        s_new = jnp.exp(last[:, :, 0, :])[..., None] * s + jnp.einsum('bhsk,bhsv->bhkv', kc * jnp.exp(last - cum), vc)
        return s_new, o
    s_fin, o = lax.scan(step, s0.astype(jnp.float32), (to_chunks(q), to_chunks(k), to_chunks(v), to_chunks(logf)))
    o = o.transpose(1, 2, 0, 3, 4).reshape(b, h, t, v.shape[-1])
    return o, s_fin


def hgrn2_bidir(q, i, zf, zb, g, lb_f, lb_b, s0_f, s0_b, norm_g):
    bsz, t, _ = q.shape
    def heads(a):
        return a.astype(jnp.float32).reshape(bsz, t, N_HEADS_A, HEAD_DIM).transpose(0, 2, 1, 3)
    qh, ih = heads(q), heads(i)
    def gates(z, lb):
        lb = lb.reshape(1, N_HEADS_A, 1, HEAD_DIM)
        f = lb + (1.0 - lb) * jax.nn.sigmoid(heads(z))
        return 1.0 - f, jnp.log(f)
    kf, lff = gates(zf, lb_f)
    kb, lfb = gates(zb, lb_b)
    o_f, s_f = hgrn_chunk_scan(qh, kf, ih, lff, s0_f)
    rev = lambda a: jnp.flip(a, axis=2)
    o_b, s_b = hgrn_chunk_scan(rev(qh), rev(kb), rev(ih), rev(lfb), s0_b)
    o = (o_f + rev(o_b)).transpose(0, 2, 1, 3)
    o = o * lax.rsqrt(jnp.mean(o * o, axis=-1, keepdims=True) + EPS)
    o = o.reshape(bsz, t, W_A) * norm_g.astype(jnp.float32) * jax.nn.silu(g.astype(jnp.float32))
    return o.astype(q.dtype), s_f.astype(q.dtype), s_b.astype(q.dtype)


def chunk_sgu(u, v, sgu_norm, w_sgu, b_sgu):
    bsz, t, _ = u.shape
    u = jax.nn.gelu(u)
    v = rmsnorm(jax.nn.gelu(v), sgu_norm)
    vh = v.reshape(bsz, t // CHUNK_B, CHUNK_B, N_HEADS_B, HEAD_DIM)
    mixed = jnp.einsum('hps,bnshc->bnphc', w_sgu, vh) + b_sgu.T[None, None, :, :, None]
    return u * mixed.reshape(bsz, t, W_B)


def centred_window_mean(x, w):
    bsz, t, ch = x.shape
    csum = jnp.concatenate([jnp.zeros((bsz, 1, ch), jnp.float32), jnp.cumsum(x.astype(jnp.float32), axis=1)], axis=1)
    pos = jnp.arange(t)
    lo = jnp.clip(pos - w // 2, 0, t)
    hi = jnp.clip(pos + w // 2, 0, t)
    s = jnp.take(csum, hi, axis=1) - jnp.take(csum, lo, axis=1)
    cnt = (hi - lo).astype(jnp.float32)[None, :, None]
    return (s / cnt).astype(x.dtype)


def multiscale_pool(x, w_pool, pool_scale):
    outs = []
    for gi, win in enumerate(POOL_WINDOWS):
        xg = x[..., gi * GROUP_D:(gi + 1) * GROUP_D]
        outs.append((centred_window_mean(xg, win) - xg) @ w_pool[gi])
    return jnp.concatenate(outs, axis=-1) * pool_scale


def token_mixers(h, s0_f, s0_b, p):
    z = h @ p['w_in']
    a_q, a_i, a_ff, a_fb, a_g, b_u, b_v, c_b, c_c, c_h, d_x = jnp.split(z, SPLIT_POINTS, axis=-1)
    a_out, s_f, s_b = hgrn2_bidir(a_q, a_i, a_ff, a_fb, a_g, p['lb_f'], p['lb_b'], s0_f, s0_b, p['hgrn_norm'])
    b_out = chunk_sgu(b_u, b_v, p['sgu_norm'], p['w_sgu'], p['b_sgu'])
    c_out = c_b * dwconv3(c_c * c_h, p['w_conv_c'])
    d_out = multiscale_pool(d_x, p['w_pool'], p['pool_scale'])
    y = jnp.concatenate([a_out, b_out, c_out, d_out], axis=-1) @ p['w_out']
    return y, s_f, s_b


def conv_ffn(h, w_up, w_conv, w_down):
    a = dwconv3(h @ w_up, w_conv)
    val, gate = jnp.split(a, 2, axis=-1)
    return (val * jax.nn.silu(gate)) @ w_down


def trunk_layer(x, mod, s0_f, s0_b, p):
    sh1, sc1, g1, sh2, sc2, g2 = mod
    h = rmsnorm(x, p['norm_mix']) * (1.0 + sc1) + sh1
    m, s_f, s_b = token_mixers(h, s0_f, s0_b, p)
    x = x + g1 * m
    h = rmsnorm(x, p['norm_ffn']) * (1.0 + sc2) + sh2
    x = x + g2 * conv_ffn(h, p['w_up'], p['w_conv_ffn'], p['w_down'])
    return x, s_f, s_b


def setup_inputs(seed: int = 0) -> dict:
    key = jax.random.key(seed)
    ks = jax.random.split(key, 24)
    f32 = jnp.float32
    def nrm(k, shape, scale):
        return jax.random.normal(k, shape, f32) * scale
    return {
        'x_prompt': nrm(ks[0], (BATCH, SEQ, D_MODEL), 1.0),
        'x_sample': nrm(ks[1], (DEC_BATCH, DEC_SEQ, D_MODEL), 1.0),
        'c': nrm(ks[2], (DEC_BATCH, D_MODEL), 1.0),
        'state_hgrn': nrm(ks[3], (DEC_BATCH, DEPTH, 2, N_HEADS_A, HEAD_DIM, HEAD_DIM), 0.5),
        'c_ctx': nrm(ks[4], (D_MODEL,), 1.0),
        'w_ada': nrm(ks[5], (DEPTH, D_MODEL, 6 * D_MODEL), D_MODEL ** -0.5),
        'b_ada': nrm(ks[6], (DEPTH, 6 * D_MODEL), 0.02),
        'norm_mix': 1.0 + nrm(ks[7], (DEPTH, D_MODEL), 0.02),
        'norm_ffn': 1.0 + nrm(ks[8], (DEPTH, D_MODEL), 0.02),
        'w_in': nrm(ks[9], (DEPTH, D_MODEL, D_IN), D_MODEL ** -0.5),
        'lb_logits': nrm(ks[10], (DEPTH, 2, W_A), 0.5),
        'hgrn_norm': 1.0 + nrm(ks[11], (DEPTH, W_A), 0.02),
        'sgu_norm': 1.0 + nrm(ks[12], (DEPTH, W_B), 0.02),
        'w_sgu': nrm(ks[13], (DEPTH, N_HEADS_B, CHUNK_B, CHUNK_B), CHUNK_B ** -0.5),
        'b_sgu': nrm(ks[14], (DEPTH, N_HEADS_B, CHUNK_B), 0.02),
        'w_conv_c': nrm(ks[15], (DEPTH, CONV_W, W_C), CONV_W ** -0.5),
        'w_pool': nrm(ks[16], (DEPTH, N_GROUPS_D, GROUP_D, GROUP_D), GROUP_D ** -0.5),
        'pool_scale': 1.0 + nrm(ks[17], (DEPTH, W_D), 0.02),
        'w_out': nrm(ks[18], (DEPTH, D_MODEL, D_MODEL), D_MODEL ** -0.5),
        'w_up': nrm(ks[19], (DEPTH, D_MODEL, 2 * D_FF), D_MODEL ** -0.5),
        'w_conv_ffn': nrm(ks[20], (DEPTH, CONV_W, 2 * D_FF), CONV_W ** -0.5),
        'w_down': nrm(ks[21], (DEPTH, D_FF, D_MODEL), D_FF ** -0.5),
        'norm_final': 1.0 + nrm(ks[22], (D_MODEL,), 0.02),
    }


def reference(x_prompt, x_sample, c, state_hgrn, c_ctx, w_ada, b_ada, norm_mix, norm_ffn, w_in,
              lb_logits, hgrn_norm, sgu_norm, w_sgu, b_sgu, w_conv_c, w_pool, pool_scale, w_out,
              w_up, w_conv_ffn, w_down, norm_final):
    lbs = lower_bounds(lb_logits)
    xp = x_prompt
    xs = x_sample + grid_pos_embed(x_sample.shape[1], x_sample.shape[2], x_sample.dtype)[None]
    zero_state = jnp.zeros((x_prompt.shape[0], N_HEADS_A, HEAD_DIM, HEAD_DIM), jnp.float32)
    new_states = []
    for l in range(DEPTH):
        p = {
            'norm_mix': norm_mix[l], 'norm_ffn': norm_ffn[l], 'w_in': w_in[l],
            'lb_f': lbs[l, 0], 'lb_b': lbs[l, 1], 'hgrn_norm': hgrn_norm[l],
            'sgu_norm': sgu_norm[l], 'w_sgu': w_sgu[l], 'b_sgu': b_sgu[l],
            'w_conv_c': w_conv_c[l], 'w_pool': w_pool[l], 'pool_scale': pool_scale[l],
            'w_out': w_out[l], 'w_up': w_up[l], 'w_conv_ffn': w_conv_ffn[l], 'w_down': w_down[l],
        }
        mod_ctx = adaln(c_ctx[None, :], w_ada[l], b_ada[l])
        mod_lat = adaln(c, w_ada[l], b_ada[l])
        xp, s_f, s_b = trunk_layer(xp, mod_ctx, zero_state, zero_state, p)
        xs, _, _ = trunk_layer(xs, mod_lat, state_hgrn[:, l, 0], state_hgrn[:, l, 1], p)
        new_states.append(jnp.stack([s_f, s_b], axis=1))
    new_state_hgrn = jnp.stack(new_states, axis=1)
    y_prompt = rmsnorm(xp, norm_final)
    y_sample = rmsnorm(xs, norm_final)
    return (y_prompt, y_sample, new_state_hgrn)
```

```python
import functools

import numpy as np
import jax
import jax.numpy as jnp
from jax import lax
from jax.experimental import pallas as pl
from jax.experimental.pallas import tpu as pltpu

F32 = jnp.float32
BF16 = jnp.bfloat16

D_MODEL = 2048
BATCH, SEQ = 16, 256
DEC_BATCH, DEC_SEQ = 2, 1024
DEPTH = 2
GRID_W = 64
HEAD_DIM = 128
W_GROUP = D_MODEL // 4
N_HEADS = W_GROUP // HEAD_DIM
POOL_WINDOWS = (2, 4, 8, 16)
POOL_HALO = 8
CHUNK_B = 128
D_FF = 5632
D_IN = 11 * W_GROUP
EPS = 1e-6

N_CTX = BATCH * SEQ
N_LAT = DEC_BATCH * DEC_SEQ
N_TOK = N_CTX + N_LAT
MOD_ROWS = 8

TM = 1024
N_TM = N_TOK // TM
N_TM_CTX = N_CTX // TM
TN_IN = 512
TF = 256
TMIX = 256
N_TMIX_CTX = N_CTX // TMIX
TMIX_PER_LAT = DEC_SEQ // TMIX
TN_ADA = 1024
NORM_ROWS = 128

HC = 64
HC_LEVELS = 6
VMEM_LIMIT = 56 * 1024 * 1024


def _silu(x):
    return x * jax.nn.sigmoid(x)


def _dot(a, b):
    return jnp.dot(a, b, preferred_element_type=F32)


def _dot_nt(a, b):
    return lax.dot_general(a, b, (((1,), (1,)), ((), ())), preferred_element_type=F32)


def _params(*sem):
    return pltpu.CompilerParams(dimension_semantics=sem, vmem_limit_bytes=VMEM_LIMIT)


def _adaln_kernel(c_ref, w_ref, b_ref, o_ref):
    s = _silu(c_ref[...]).astype(BF16)
    o_ref[...] = _dot(s, w_ref[...].astype(BF16)) + b_ref[...]


def _adaln(cvec, w_ada, b_ada):
    n_out = w_ada.shape[-1]
    return pl.pallas_call(
        _adaln_kernel,
        grid=(DEPTH, n_out // TN_ADA),
        in_specs=[
            pl.BlockSpec((MOD_ROWS, D_MODEL), lambda l, j: (0, 0)),
            pl.BlockSpec((None, D_MODEL, TN_ADA), lambda l, j: (l, 0, j)),
            pl.BlockSpec((None, 1, TN_ADA), lambda l, j: (l, 0, j)),
        ],
        out_specs=pl.BlockSpec((None, MOD_ROWS, TN_ADA), lambda l, j: (l, 0, j)),
        out_shape=jax.ShapeDtypeStruct((DEPTH, MOD_ROWS, n_out), F32),
        compiler_params=_params("arbitrary", "arbitrary"),
        name="adaln",
    )(cvec, w_ada, b_ada.reshape(DEPTH, 1, n_out))


def _norm_mod_rows(x_ref, h_ref, gain, shift, scale, n_rows):
    def body(n, carry):
        r0 = pl.multiple_of(n * NORM_ROWS, NORM_ROWS)
        x = x_ref[pl.ds(r0, NORM_ROWS), :]
        ms = jnp.mean(x * x, axis=-1, keepdims=True)
        y = x * lax.rsqrt(ms + EPS) * gain
        h_ref[pl.ds(r0, NORM_ROWS), :] = (y * (1.0 + scale) + shift).astype(BF16)
        return carry
    lax.fori_loop(0, n_rows // NORM_ROWS, body, 0)


def _mod_row_of_big_tile(i):
    return jnp.maximum(i - (N_TM_CTX - 1), 0)


def _inproj_kernel(x_ref, g_ref, sh_ref, sc_ref, w_ref, z_ref, h_scr):
    i = pl.program_id(0)

    @pl.when(pl.program_id(1) == 0)
    def _():
        r = _mod_row_of_big_tile(i)
        _norm_mod_rows(x_ref, h_scr, g_ref[...], sh_ref[pl.ds(r, 1), :], sc_ref[pl.ds(r, 1), :], TM)

    z_ref[...] = _dot(h_scr[...], w_ref[...])


def _inproj(x, gain, mod, w_in):
    return pl.pallas_call(
        _inproj_kernel,
        grid=(N_TM, D_IN // TN_IN),
        in_specs=[
            pl.BlockSpec((TM, D_MODEL), lambda i, j: (i, 0)),
            pl.BlockSpec((1, D_MODEL), lambda i, j: (0, 0)),
            pl.BlockSpec((MOD_ROWS, D_MODEL), lambda i, j: (0, 0)),
            pl.BlockSpec((MOD_ROWS, D_MODEL), lambda i, j: (0, 1)),
            pl.BlockSpec((D_MODEL, TN_IN), lambda i, j: (0, j)),
        ],
        out_specs=pl.BlockSpec((TM, TN_IN), lambda i, j: (i, j)),
        out_shape=jax.ShapeDtypeStruct((N_TOK, D_IN), F32),
        scratch_shapes=[pltpu.VMEM((TM, D_MODEL), BF16)],
        compiler_params=_params("arbitrary", "arbitrary"),
        name="inproj",
    )(x, gain, mod, mod, w_in)


def _hgrn_constants():
    t = np.arange(HC)
    run = [(t[None, :] <= t[:, None]), (t[None, :] >= t[:, None])]
    g = np.zeros((2, HC_LEVELS + 2, HC, HC), np.float32)
    uq = np.ones((2, HC_LEVELS + 1, HC), np.float32)
    bm = np.zeros((HC_LEVELS + 1, HC, HC), np.float32)
    for d in range(2):
        g[d, 0] = run[d]
        for lev in range(HC_LEVELS):
            m = 1 << lev
            base = (t // (2 * m)) * (2 * m)
            g[d, lev + 1] = run[d][base + m - 1 + d]
            upper = (t % (2 * m)) >= m
            uq[d, lev] = upper if d == 0 else ~upper
        g[d, HC_LEVELS + 1] = run[d][np.full(HC, HC - 1 if d == 0 else 0)]
    uk = 1.0 - uq
    uk[:, HC_LEVELS] = 1.0
    for lev in range(HC_LEVELS):
        m = 1 << lev
        bm[lev] = (t[:, None] // (2 * m)) == (t[None, :] // (2 * m))
    bm[HC_LEVELS] = np.eye(HC)
    g = g.reshape(2, (HC_LEVELS + 2) * HC, HC)
    g3 = np.concatenate([g, g, g], axis=-1)
    lanes = lambda a: np.ascontiguousarray(np.broadcast_to(a[..., None], a.shape + (HEAD_DIM,)))
    return (jnp.asarray(g3, BF16), jnp.asarray(lanes(uq), F32), jnp.asarray(lanes(uk), F32),
            jnp.asarray(bm, F32))


def _hgrn_chunk(d, h, r0, z_ref, lb_ref, g3_ref, uq_ref, uk_ref, bm_ref, s_scr, o_scr):
    c0 = h * HEAD_DIM
    rows = pl.ds(r0, HC)
    q = z_ref[rows, c0:c0 + HEAD_DIM]
    v = z_ref[rows, W_GROUP + c0:W_GROUP + c0 + HEAD_DIM]
    zz = z_ref[rows, (2 + d) * W_GROUP + c0:(2 + d) * W_GROUP + c0 + HEAD_DIM]
    lb = lb_ref[d:d + 1, c0:c0 + HEAD_DIM]
    f = lb + (1.0 - lb) * jax.nn.sigmoid(zz)
    kk = 1.0 - f
    lf = jnp.log(f)

    l1 = lf.astype(BF16)
    r1 = lf - l1.astype(F32)
    l2 = r1.astype(BF16)
    l3 = (r1 - l2.astype(F32)).astype(BF16)
    sums = _dot(g3_ref[d], jnp.concatenate([l1, l2, l3], axis=0))
    cum = sums[0:HC]
    far = sums[(HC_LEVELS + 1) * HC:(HC_LEVELS + 2) * HC]

    vb = v.astype(BF16)
    att = None
    for lev in range(HC_LEVELS + 1):
        if lev < HC_LEVELS:
            ex = jnp.exp(-jnp.abs(cum - sums[(lev + 1) * HC:(lev + 2) * HC]))
            qs = q * ex * uq_ref[d, lev]
            ks = kk * ex * uk_ref[d, lev]
        else:
            qs, ks = q, kk
        sc = _dot_nt(qs.astype(BF16), ks.astype(BF16)) * bm_ref[lev]
        att = sc if att is None else att + sc

    s_old = s_scr[d * N_HEADS + h]
    q_in = (q * jnp.exp(cum)).astype(BF16)
    o_scr[rows, c0:c0 + HEAD_DIM] = _dot(att.astype(BF16), vb) + _dot(q_in, s_old.astype(BF16))

    k_out = kk * jnp.exp(-jnp.abs(cum - far))
    k_ext = jnp.concatenate([k_out, jnp.exp(far[0:8])], axis=0)
    k_ext_t = k_ext.T
    s_scr[d * N_HEADS + h] = k_ext_t[:, HC:HC + 1] * s_old + _dot(k_ext_t[:, 0:HC].astype(BF16), vb)


def _hgrn_kernel(*refs, n_chunks, has_s0, want_state):
    z_ref, lb_ref, ng_ref, g3_ref, uq_ref, uk_ref, bm_ref = refs[:7]
    k = 7
    s0_ref = st_ref = None
    if has_s0:
        s0_ref = refs[k]
        k += 1
    a_ref = refs[k]
    k += 1
    if want_state:
        st_ref = refs[k]
        k += 1
    s_scr, of_scr, ob_scr = refs[k:k + 3]

    for d in range(2):
        for h in range(N_HEADS):
            s_scr[d * N_HEADS + h] = (s0_ref[d, h] if has_s0
                                      else jnp.zeros((HEAD_DIM, HEAD_DIM), F32))

    def scan_body(n, carry):
        for d, o_scr in ((0, of_scr), (1, ob_scr)):
            r0 = pl.multiple_of((n if d == 0 else n_chunks - 1 - n) * HC, HC)
            for h in range(N_HEADS):
                _hgrn_chunk(d, h, r0, z_ref, lb_ref, g3_ref, uq_ref, uk_ref, bm_ref, s_scr, o_scr)
        return carry
    lax.fori_loop(0, n_chunks, scan_body, 0)

    def out_body(n, carry):
        rows = pl.ds(pl.multiple_of(n * HC, HC), HC)
        o = of_scr[rows, :] + ob_scr[rows, :]
        parts = []
        for h in range(N_HEADS):
            oh = o[:, h * HEAD_DIM:(h + 1) * HEAD_DIM]
            parts.append(oh * lax.rsqrt(jnp.mean(oh * oh, axis=-1, keepdims=True) + EPS))
        gate = z_ref[rows, 4 * W_GROUP:5 * W_GROUP]
        a_ref[rows, :] = jnp.concatenate(parts, axis=-1) * ng_ref[...] * _silu(gate)
        return carry
    lax.fori_loop(0, n_chunks, out_body, 0)

    if want_state:
        for d in range(2):
            for h in range(N_HEADS):
                st_ref[d, h] = s_scr[d * N_HEADS + h]


def _hgrn(z, lb, norm_g, consts, *, seq_len, n_seq, row_block0, s0=None, layer=0, a_prev=None):
    has_s0 = s0 is not None
    want_state = not has_s0
    const_specs = [pl.BlockSpec(c.shape, lambda b, nd=c.ndim: (0,) * nd) for c in consts]
    in_specs = [
        pl.BlockSpec((seq_len, 5 * W_GROUP), lambda b: (row_block0 + b, 0)),
        pl.BlockSpec((2, W_GROUP), lambda b: (0, 0)),
        pl.BlockSpec((1, W_GROUP), lambda b: (0, 0)),
    ] + const_specs
    args = [z, lb, norm_g, *consts]
    if has_s0:
        in_specs.append(pl.BlockSpec((None, None, 2, N_HEADS, HEAD_DIM, HEAD_DIM),
                                     lambda b: (b, layer, 0, 0, 0, 0)))
        args.append(s0)
    out_specs = [pl.BlockSpec((seq_len, W_GROUP), lambda b: (row_block0 + b, 0))]
    out_shape = [jax.ShapeDtypeStruct((N_TOK, W_GROUP), F32)]
    if want_state:
        out_specs.append(pl.BlockSpec((None, 2, N_HEADS, HEAD_DIM, HEAD_DIM),
                                      lambda b: (b, 0, 0, 0, 0)))
        out_shape.append(jax.ShapeDtypeStruct((n_seq, 2, N_HEADS, HEAD_DIM, HEAD_DIM), F32))
    aliases = {}
    if a_prev is not None:
        in_specs.append(pl.BlockSpec(memory_space=pl.ANY))
        args.append(a_prev)
        aliases = {len(args) - 1: 0}
    kern = functools.partial(_hgrn_kernel_aliased if a_prev is not None else _hgrn_kernel,
                             n_chunks=seq_len // HC, has_s0=has_s0, want_state=want_state)
    return pl.pallas_call(
        kern,
        grid=(n_seq,),
        in_specs=in_specs,
        out_specs=out_specs,
        out_shape=out_shape,
        scratch_shapes=[
            pltpu.VMEM((2 * N_HEADS, HEAD_DIM, HEAD_DIM), F32),
            pltpu.VMEM((seq_len, W_GROUP), F32),
            pltpu.VMEM((seq_len, W_GROUP), F32),
        ],
        input_output_aliases=aliases,
        compiler_params=_params("arbitrary"),
        name="hgrn_lat" if has_s0 else "hgrn_ctx",
    )(*args)


def _hgrn_kernel_aliased(*refs, n_chunks, has_s0, want_state):
    n_in = 7 + int(has_s0)
    _hgrn_kernel(*refs[:n_in], *refs[n_in + 1:], n_chunks=n_chunks, has_s0=has_s0,
                 want_state=want_state)


def _mix_kernel(a_ref, bu_ref, bv_ref, cb_ref, cc_ref, ch_ref, dx_ref,
                ccp_ref, chp_ref, dxp_ref, ccn_ref, chn_ref, dxn_ref,
                x_ref, g1_ref, sgun_ref, wsgu_ref, bsgu_ref, wconv_ref, wpool_ref, pscale_ref,
                wout_ref, o_ref, cat_scr):
    i = pl.program_id(0)
    is_ctx = i < N_TMIX_CTX
    lat_tile = jnp.maximum(i - N_TMIX_CTX, 0)
    j = jnp.where(is_ctx, 0, lat_tile % TMIX_PER_LAT)
    tiles_in_seq = jnp.where(is_ctx, 1, TMIX_PER_LAT)
    keep_prev = jnp.where(j == 0, 0.0, 1.0)
    keep_next = jnp.where(j == tiles_in_seq - 1, 0.0, 1.0)
    mod_row = jnp.where(is_ctx, 0, 1 + lat_tile // TMIX_PER_LAT)
    row = lax.broadcasted_iota(jnp.int32, (TMIX, 1), 0)

    cat_scr[:, 0:W_GROUP] = a_ref[...].astype(BF16)

    u = jax.nn.gelu(bu_ref[...])
    vv = jax.nn.gelu(bv_ref[...])
    vn = (vv * lax.rsqrt(jnp.mean(vv * vv, axis=-1, keepdims=True) + EPS) * sgun_ref[...]).astype(BF16)
    for n in range(TMIX // CHUNK_B):
        rs = slice(n * CHUNK_B, (n + 1) * CHUNK_B)
        for h in range(N_HEADS):
            cs = slice(h * HEAD_DIM, (h + 1) * HEAD_DIM)
            mixed = _dot(wsgu_ref[h], vn[rs, cs]) + bsgu_ref[:, cs]
            cat_scr[rs, W_GROUP + h * HEAD_DIM:W_GROUP + (h + 1) * HEAD_DIM] = (u[rs, cs] * mixed).astype(BF16)

    p = cc_ref[...] * ch_ref[...]
    p_edge_prev = ccp_ref[POOL_HALO - 1:POOL_HALO, :] * chp_ref[POOL_HALO - 1:POOL_HALO, :] * keep_prev
    p_edge_next = ccn_ref[0:1, :] * chn_ref[0:1, :] * keep_next
    p_prev = jnp.where(row == 0, p_edge_prev, pltpu.roll(p, 1, 0))
    p_next = jnp.where(row == TMIX - 1, p_edge_next, pltpu.roll(p, TMIX - 1, 0))
    wc = wconv_ref[...]
    c_out = cb_ref[...] * (wc[0:1] * p_prev + wc[1:2] * p + wc[2:3] * p_next)
    cat_scr[:, 2 * W_GROUP:3 * W_GROUP] = c_out.astype(BF16)

    xd = dx_ref[...]
    ext = jnp.concatenate([dxp_ref[...] * keep_prev, xd, dxn_ref[...] * keep_next], axis=0)
    n_ext = TMIX + 2 * POOL_HALO
    back = lambda a, s: pltpu.roll(a, s, 0)
    ahead = lambda a, s: pltpu.roll(a, n_ext - s, 0)
    sums = [back(ext, 1) + ext]
    for s in (1, 2, 4):
        sums.append(back(sums[-1], s) + ahead(sums[-1], s))
    pos = j * TMIX + row
    seq_len = tiles_in_seq * TMIX
    gd = W_GROUP // len(POOL_WINDOWS)
    for g, win in enumerate(POOL_WINDOWS):
        cs = slice(g * gd, (g + 1) * gd)
        cnt = (jnp.minimum(pos + win // 2, seq_len) - jnp.maximum(pos - win // 2, 0)).astype(F32)
        mean = sums[g][POOL_HALO:POOL_HALO + TMIX, cs] / cnt
        dg = _dot((mean - xd[:, cs]).astype(BF16), wpool_ref[g]) * pscale_ref[:, cs]
        cat_scr[:, 3 * W_GROUP + g * gd:3 * W_GROUP + (g + 1) * gd] = dg.astype(BF16)

    y = _dot(cat_scr[...], wout_ref[...])
    o_ref[...] = x_ref[...] + g1_ref[pl.ds(mod_row, 1), :] * y


def _mix(a, z, x, mod, sgu_norm, w_sgu, b_sgu_full, w_conv, w_pool, pool_scale, w_out):
    halo_blocks = TMIX // POOL_HALO
    last_halo = N_TOK // POOL_HALO - 1
    zcol = lambda k: pl.BlockSpec((TMIX, W_GROUP), lambda i: (i, k))
    zprev = lambda k: pl.BlockSpec((POOL_HALO, W_GROUP),
                                   lambda i: (jnp.maximum(i * halo_blocks - 1, 0), k))
    znext = lambda k: pl.BlockSpec((POOL_HALO, W_GROUP),
                                   lambda i: (jnp.minimum((i + 1) * halo_blocks, last_halo), k))
    full = lambda arr: pl.BlockSpec(arr.shape, lambda i, nd=arr.ndim: (0,) * nd)
    return pl.pallas_call(
        _mix_kernel,
        grid=(N_TOK // TMIX,),
        in_specs=[
            pl.BlockSpec((TMIX, W_GROUP), lambda i: (i, 0)),
            zcol(5), zcol(6), zcol(7), zcol(8), zcol(9), zcol(10),
            zprev(8), zprev(9), zprev(10), znext(8), znext(9), znext(10),
            pl.BlockSpec((TMIX, D_MODEL), lambda i: (i, 0)),
            pl.BlockSpec((MOD_ROWS, D_MODEL), lambda i: (0, 2)),
            full(sgu_norm), full(w_sgu), full(b_sgu_full), full(w_conv), full(w_pool),
            full(pool_scale), full(w_out),
        ],
        out_specs=pl.BlockSpec((TMIX, D_MODEL), lambda i: (i, 0)),
        out_shape=jax.ShapeDtypeStruct((N_TOK, D_MODEL), F32),
        scratch_shapes=[pltpu.VMEM((TMIX, D_MODEL), BF16)],
        compiler_params=_params("arbitrary"),
        name="mix",
    )(a, z, z, z, z, z, z, z, z, z, z, z, z, x, mod, sgu_norm, w_sgu, b_sgu_full, w_conv, w_pool,
      pool_scale, w_out)


def _ffn_kernel(x_ref, g_ref, sh_ref, sc_ref, g2_ref, wv_ref, wg_ref, cv_ref, cg_ref, wd_ref,
                o_ref, h_scr):
    i = pl.program_id(0)
    j = pl.program_id(1)
    r = _mod_row_of_big_tile(i)

    @pl.when(j == 0)
    def _():
        _norm_mod_rows(x_ref, h_scr, g_ref[...], sh_ref[pl.ds(r, 1), :], sc_ref[pl.ds(r, 1), :], TM)

    seq_mask = jnp.where(i < N_TM_CTX, SEQ - 1, DEC_SEQ - 1)
    t_in_seq = lax.broadcasted_iota(jnp.int32, (TM, 1), 0) & seq_mask
    keep_prev = (t_in_seq != 0).astype(F32)
    keep_next = (t_in_seq != seq_mask).astype(F32)

    def conv3(u, w):
        return (w[0:1] * (pltpu.roll(u, 1, 0) * keep_prev) + w[1:2] * u
                + w[2:3] * (pltpu.roll(u, TM - 1, 0) * keep_next))

    h = h_scr[...]
    val = conv3(_dot(h, wv_ref[...]), cv_ref[...])
    gate = conv3(_dot(h, wg_ref[...]), cg_ref[...])
    act = (val * _silu(gate)).astype(BF16)

    n_col = 512
    for c in range(D_MODEL // n_col):
        cs = slice(c * n_col, (c + 1) * n_col)
        part = _dot(act, wd_ref[:, cs])

        @pl.when(j == 0)
        def _():
            o_ref[:, cs] = part

        @pl.when(j > 0)
        def _():
            o_ref[:, cs] += part

    @pl.when(j == pl.num_programs(1) - 1)
    def _():
        g2 = g2_ref[pl.ds(r, 1), :]

        def body(n, carry):
            rows = pl.ds(pl.multiple_of(n * NORM_ROWS, NORM_ROWS), NORM_ROWS)
            o_ref[rows, :] = x_ref[rows, :] + g2 * o_ref[rows, :]
            return carry
        lax.fori_loop(0, TM // NORM_ROWS, body, 0)


def _ffn(x, gain, mod, w_up, w_conv, w_down):
    n_f = D_FF // TF
    return pl.pallas_call(
        _ffn_kernel,
        grid=(N_TM, n_f),
        in_specs=[
            pl.BlockSpec((TM, D_MODEL), lambda i, j: (i, 0)),
            pl.BlockSpec((1, D_MODEL), lambda i, j: (0, 0)),
            pl.BlockSpec((MOD_ROWS, D_MODEL), lambda i, j: (0, 3)),
            pl.BlockSpec((MOD_ROWS, D_MODEL), lambda i, j: (0, 4)),
            pl.BlockSpec((MOD_ROWS, D_MODEL), lambda i, j: (0, 5)),
            pl.BlockSpec((D_MODEL, TF), lambda i, j: (0, j)),
            pl.BlockSpec((D_MODEL, TF), lambda i, j: (0, j + n_f)),
            pl.BlockSpec((3, TF), lambda i, j: (0, j)),
            pl.BlockSpec((3, TF), lambda i, j: (0, j + n_f)),
            pl.BlockSpec((TF, D_MODEL), lambda i, j: (j, 0)),
        ],
        out_specs=pl.BlockSpec((TM, D_MODEL), lambda i, j: (i, 0)),
        out_shape=jax.ShapeDtypeStruct((N_TOK, D_MODEL), F32),
        scratch_shapes=[pltpu.VMEM((TM, D_MODEL), BF16)],
        compiler_params=_params("arbitrary", "arbitrary"),
        name="ffn",
    )(x, gain, mod, mod, mod, w_up, w_up, w_conv, w_conv, w_down)


def _final_norm_kernel(x_ref, g_ref, o_ref):
    gain = g_ref[...]

    def body(n, carry):
        rows = pl.ds(pl.multiple_of(n * NORM_ROWS, NORM_ROWS), NORM_ROWS)
        x = x_ref[rows, :]
        o_ref[rows, :] = x * lax.rsqrt(jnp.mean(x * x, axis=-1, keepdims=True) + EPS) * gain
        return carry
    lax.fori_loop(0, TM // NORM_ROWS, body, 0)


def _final_norm(x, gain, tile0, n_tiles):
    return pl.pallas_call(
        _final_norm_kernel,
        grid=(n_tiles,),
        in_specs=[pl.BlockSpec((TM, D_MODEL), lambda i: (tile0 + i, 0)),
                  pl.BlockSpec((1, D_MODEL), lambda i: (0, 0))],
        out_specs=pl.BlockSpec((TM, D_MODEL), lambda i: (i, 0)),
        out_shape=jax.ShapeDtypeStruct((n_tiles * TM, D_MODEL), F32),
        compiler_params=_params("arbitrary"),
        name="final_norm",
    )(x, gain)


def _grid_pos_embed():
    rows = DEC_SEQ // GRID_W
    quarter = D_MODEL // 4
    freq = jnp.exp(-jnp.log(10000.0) * jnp.arange(quarter, dtype=F32) / quarter)[None, :]
    r = jnp.arange(rows, dtype=F32)[:, None] * freq
    col = jnp.arange(GRID_W, dtype=F32)[:, None] * freq
    rep = lambda a: jnp.repeat(a, GRID_W, axis=0)
    til = lambda a: jnp.tile(a, (rows, 1))
    return jnp.concatenate([rep(jnp.sin(r)), rep(jnp.cos(r)), til(jnp.sin(col)), til(jnp.cos(col))], -1)


def kernel(x_prompt, x_sample, c, state_hgrn, c_ctx, w_ada, b_ada, norm_mix, norm_ffn, w_in,
           lb_logits, hgrn_norm, sgu_norm, w_sgu, b_sgu, w_conv_c, w_pool, pool_scale, w_out,
           w_up, w_conv_ffn, w_down, norm_final):
    p = jax.nn.softmax(lb_logits.astype(F32), axis=0)
    lbs = jnp.cumsum(p, axis=0) - p[0:1]

    x = jnp.concatenate([x_prompt.reshape(N_CTX, D_MODEL),
                         (x_sample + _grid_pos_embed()[None]).reshape(N_LAT, D_MODEL)], axis=0)
    cvec = jnp.concatenate([c_ctx[None, :], c, jnp.zeros((MOD_ROWS - 1 - DEC_BATCH, D_MODEL), F32)], 0)
    mod = _adaln(cvec, w_ada, b_ada)

    consts = _hgrn_constants()
    states = []
    for l in range(DEPTH):
        row = lambda a: a[l].reshape(1, -1)
        z = _inproj(x, row(norm_mix), mod[l], w_in[l].astype(BF16))
        a_ctx, st = _hgrn(z, lbs[l], row(hgrn_norm), consts, seq_len=SEQ, n_seq=BATCH, row_block0=0)
        a = _hgrn(z, lbs[l], row(hgrn_norm), consts, seq_len=DEC_SEQ, n_seq=DEC_BATCH,
                  row_block0=N_CTX // DEC_SEQ, s0=state_hgrn, layer=l, a_prev=a_ctx)[0]
        b_sgu_full = jnp.repeat(b_sgu[l].T, HEAD_DIM, axis=1)
        x = _mix(a, z, x, mod[l], row(sgu_norm), w_sgu[l].astype(BF16), b_sgu_full, w_conv_c[l],
                 w_pool[l].astype(BF16), row(pool_scale), w_out[l].astype(BF16))
        x = _ffn(x, row(norm_ffn), mod[l], w_up[l].astype(BF16), w_conv_ffn[l], w_down[l].astype(BF16))
        states.append(st)

    gain = norm_final.reshape(1, -1)
    y_prompt = _final_norm(x, gain, 0, N_TM_CTX).reshape(BATCH, SEQ, D_MODEL)
    y_sample = _final_norm(x, gain, N_TM_CTX, N_TM - N_TM_CTX).reshape(DEC_BATCH, DEC_SEQ, D_MODEL)
    return (y_prompt, y_sample, jnp.stack(states, axis=1))
```

```python
import functools

import numpy as np
import jax
import jax.numpy as jnp
from jax import lax
from jax.experimental import pallas as pl
from jax.experimental.pallas import tpu as pltpu

F32 = jnp.float32
BF16 = jnp.bfloat16

D_MODEL = 2048
BATCH, SEQ = 16, 256
DEC_BATCH, DEC_SEQ = 2, 1024
DEPTH = 2
GRID_W = 64
HEAD_DIM = 128
W_GROUP = D_MODEL // 4
N_HEADS = W_GROUP // HEAD_DIM
POOL_WINDOWS = (2, 4, 8, 16)
POOL_HALO = 8
CHUNK_B = 128
D_FF = 5632
D_IN = 11 * W_GROUP
EPS = 1e-6

N_CTX = BATCH * SEQ
N_LAT = DEC_BATCH * DEC_SEQ
N_TOK = N_CTX + N_LAT
MOD_ROWS = 8

TM = 1024
N_TM = N_TOK // TM
N_TM_CTX = N_CTX // TM
TN_IN = 512
TF = 512
FFN_CT = 256
FFN_RC = 64
FFN_NOUT = 512
TMIX = 256
N_TMIX_CTX = N_CTX // TMIX
TMIX_PER_LAT = DEC_SEQ // TMIX
TN_ADA = 1024
NORM_ROWS = 128

HC = 64
HC_LEVELS = 6
VMEM_LIMIT = 56 * 1024 * 1024


def _silu(x):
    return x * jax.nn.sigmoid(x)


def _dot(a, b):
    return jnp.dot(a, b, preferred_element_type=F32)


def _dot_nt(a, b):
    return lax.dot_general(a, b, (((1,), (1,)), ((), ())), preferred_element_type=F32)


def _params(*sem):
    return pltpu.CompilerParams(dimension_semantics=sem, vmem_limit_bytes=VMEM_LIMIT)


def _adaln_kernel(c_ref, w_ref, b_ref, o_ref):
    s = _silu(c_ref[...]).astype(BF16)
    o_ref[...] = _dot(s, w_ref[...].astype(BF16)) + b_ref[...]


def _adaln(cvec, w_ada, b_ada):
    n_out = w_ada.shape[-1]
    return pl.pallas_call(
        _adaln_kernel,
        grid=(DEPTH, n_out // TN_ADA),
        in_specs=[
            pl.BlockSpec((MOD_ROWS, D_MODEL), lambda l, j: (0, 0)),
            pl.BlockSpec((None, D_MODEL, TN_ADA), lambda l, j: (l, 0, j)),
            pl.BlockSpec((None, 1, TN_ADA), lambda l, j: (l, 0, j)),
        ],
        out_specs=pl.BlockSpec((None, MOD_ROWS, TN_ADA), lambda l, j: (l, 0, j)),
        out_shape=jax.ShapeDtypeStruct((DEPTH, MOD_ROWS, n_out), F32),
        compiler_params=_params("arbitrary", "arbitrary"),
        name="adaln",
    )(cvec, w_ada, b_ada.reshape(DEPTH, 1, n_out))


def _norm_mod_rows(x_ref, h_ref, gain, shift, scale, n_rows):
    gain_eff = gain * (1.0 + scale)

    def body(n, carry):
        r0 = pl.multiple_of(n * NORM_ROWS, NORM_ROWS)
        x = x_ref[pl.ds(r0, NORM_ROWS), :]
        ms = jnp.mean(x * x, axis=-1, keepdims=True)
        h_ref[pl.ds(r0, NORM_ROWS), :] = (x * lax.rsqrt(ms + EPS) * gain_eff + shift).astype(BF16)
        return carry
    lax.fori_loop(0, n_rows // NORM_ROWS, body, 0)


def _mod_row_of_big_tile(i):
    return jnp.maximum(i - (N_TM_CTX - 1), 0)


def _inproj_kernel(x_ref, g_ref, sh_ref, sc_ref, w_ref, z_ref, h_scr):
    i = pl.program_id(0)

    @pl.when(pl.program_id(1) == 0)
    def _():
        r = _mod_row_of_big_tile(i)
        _norm_mod_rows(x_ref, h_scr, g_ref[...], sh_ref[pl.ds(r, 1), :], sc_ref[pl.ds(r, 1), :], TM)

    z_ref[...] = _dot(h_scr[...], w_ref[...].astype(BF16))


def _mod_spec(layer, k):
    return pl.BlockSpec((None, MOD_ROWS, D_MODEL), lambda *_: (layer, 0, k))


def _inproj(x, gain, mod, w_in, layer):
    return pl.pallas_call(
        _inproj_kernel,
        grid=(N_TM, D_IN // TN_IN),
        in_specs=[
            pl.BlockSpec((TM, D_MODEL), lambda i, j: (i, 0)),
            pl.BlockSpec((1, D_MODEL), lambda i, j: (0, 0)),
            _mod_spec(layer, 0),
            _mod_spec(layer, 1),
            pl.BlockSpec((None, D_MODEL, TN_IN), lambda i, j: (layer, 0, j)),
        ],
        out_specs=pl.BlockSpec((TM, TN_IN), lambda i, j: (i, j)),
        out_shape=jax.ShapeDtypeStruct((N_TOK, D_IN), F32),
        scratch_shapes=[pltpu.VMEM((TM, D_MODEL), BF16)],
        compiler_params=_params("arbitrary", "arbitrary"),
        name="inproj",
    )(x, gain, mod, mod, w_in)


def _hgrn_constants():
    t = np.arange(HC)
    run = [(t[None, :] <= t[:, None]), (t[None, :] >= t[:, None])]
    g = np.zeros((2, HC_LEVELS + 2, HC, HC), np.float32)
    uq = np.ones((2, HC_LEVELS + 1, HC), np.float32)
    bm = np.zeros((HC_LEVELS + 1, HC, HC), np.float32)
    for d in range(2):
        g[d, 0] = run[d]
        for lev in range(HC_LEVELS):
            m = 1 << lev
            base = (t // (2 * m)) * (2 * m)
            g[d, lev + 1] = run[d][base + m - 1 + d]
            upper = (t % (2 * m)) >= m
            uq[d, lev] = upper if d == 0 else ~upper
        g[d, HC_LEVELS + 1] = run[d][np.full(HC, HC - 1 if d == 0 else 0)]
    uk = 1.0 - uq
    uk[:, HC_LEVELS] = 1.0
    for lev in range(HC_LEVELS):
        m = 1 << lev
        bm[lev] = (t[:, None] // (2 * m)) == (t[None, :] // (2 * m))
    bm[HC_LEVELS] = np.eye(HC)
    g = g.reshape(2, (HC_LEVELS + 2) * HC, HC)
    g3 = np.concatenate([g, g, g], axis=-1)
    lanes = lambda a: np.ascontiguousarray(np.broadcast_to(a[..., None], a.shape + (HEAD_DIM,)))
    return (jnp.asarray(g3, BF16), jnp.asarray(lanes(uq), F32), jnp.asarray(lanes(uk), F32),
            jnp.asarray(bm, F32))


def _hgrn_chunk(d, h, r0, z_ref, lb_ref, g3_ref, uq_ref, uk_ref, bm_ref, s_scr, o_scr):
    c0 = h * HEAD_DIM
    rows = pl.ds(r0, HC)
    q = z_ref[rows, c0:c0 + HEAD_DIM]
    v = z_ref[rows, W_GROUP + c0:W_GROUP + c0 + HEAD_DIM]
    zz = z_ref[rows, (2 + d) * W_GROUP + c0:(2 + d) * W_GROUP + c0 + HEAD_DIM]
    lb = lb_ref[d:d + 1, c0:c0 + HEAD_DIM]
    f = lb + (1.0 - lb) * jax.nn.sigmoid(zz)
    kk = 1.0 - f
    lf = jnp.log(f)

    l1 = lf.astype(BF16)
    r1 = lf - l1.astype(F32)
    l2 = r1.astype(BF16)
    l3 = (r1 - l2.astype(F32)).astype(BF16)
    sums = _dot(g3_ref[d], jnp.concatenate([l1, l2, l3], axis=0))
    cum = sums[0:HC]
    far = sums[(HC_LEVELS + 1) * HC:(HC_LEVELS + 2) * HC]

    vb = v.astype(BF16)
    att = None
    for lev in range(HC_LEVELS + 1):
        if lev < HC_LEVELS:
            ex = jnp.exp(-jnp.abs(cum - sums[(lev + 1) * HC:(lev + 2) * HC]))
            qs = q * ex * uq_ref[d, lev]
            ks = kk * ex * uk_ref[d, lev]
        else:
            qs, ks = q, kk
        sc = _dot_nt(qs.astype(BF16), ks.astype(BF16)) * bm_ref[lev]
        att = sc if att is None else att + sc

    s_old = s_scr[d * N_HEADS + h]
    q_in = (q * jnp.exp(cum)).astype(BF16)
    o_scr[rows, c0:c0 + HEAD_DIM] = _dot(att.astype(BF16), vb) + _dot(q_in, s_old.astype(BF16))

    k_out = kk * jnp.exp(-jnp.abs(cum - far))
    k_ext = jnp.concatenate([k_out, jnp.exp(far[0:8])], axis=0)
    k_ext_t = k_ext.T
    s_scr[d * N_HEADS + h] = k_ext_t[:, HC:HC + 1] * s_old + _dot(k_ext_t[:, 0:HC].astype(BF16), vb)


def _hgrn_kernel(*refs, n_chunks, has_s0, want_state):
    z_ref, lb_ref, ng_ref, g3_ref, uq_ref, uk_ref, bm_ref = refs[:7]
    k = 7
    s0_ref = st_ref = None
    if has_s0:
        s0_ref = refs[k]
        k += 1
    a_ref = refs[k]
    k += 1
    if want_state:
        st_ref = refs[k]
        k += 1
    s_scr, of_scr, ob_scr = refs[k:k + 3]

    for d in range(2):
        for h in range(N_HEADS):
            s_scr[d * N_HEADS + h] = (s0_ref[d, h] if has_s0
                                      else jnp.zeros((HEAD_DIM, HEAD_DIM), F32))

    def scan_body(n, carry):
        for d, o_scr in ((0, of_scr), (1, ob_scr)):
            r0 = pl.multiple_of((n if d == 0 else n_chunks - 1 - n) * HC, HC)
            for h in range(N_HEADS):
                _hgrn_chunk(d, h, r0, z_ref, lb_ref, g3_ref, uq_ref, uk_ref, bm_ref, s_scr, o_scr)
        return carry
    lax.fori_loop(0, n_chunks, scan_body, 0)

    def out_body(n, carry):
        rows = pl.ds(pl.multiple_of(n * HC, HC), HC)
        o = of_scr[rows, :] + ob_scr[rows, :]
        parts = []
        for h in range(N_HEADS):
            oh = o[:, h * HEAD_DIM:(h + 1) * HEAD_DIM]
            parts.append(oh * lax.rsqrt(jnp.mean(oh * oh, axis=-1, keepdims=True) + EPS))
        gate = z_ref[rows, 4 * W_GROUP:5 * W_GROUP]
        a_ref[rows, :] = jnp.concatenate(parts, axis=-1) * ng_ref[...] * _silu(gate)
        return carry
    lax.fori_loop(0, n_chunks, out_body, 0)

    if want_state:
        for d in range(2):
            for h in range(N_HEADS):
                st_ref[d, h] = s_scr[d * N_HEADS + h]


def _hgrn(z, lb, norm_g, consts, *, seq_len, n_seq, row_block0, s0=None, layer=0, a_prev=None):
    has_s0 = s0 is not None
    want_state = not has_s0
    const_specs = [pl.BlockSpec(c.shape, lambda b, nd=c.ndim: (0,) * nd) for c in consts]
    in_specs = [
        pl.BlockSpec((seq_len, 5 * W_GROUP), lambda b: (row_block0 + b, 0)),
        pl.BlockSpec((2, W_GROUP), lambda b: (0, 0)),
        pl.BlockSpec((1, W_GROUP), lambda b: (0, 0)),
    ] + const_specs
    args = [z, lb, norm_g, *consts]
    if has_s0:
        in_specs.append(pl.BlockSpec((None, None, 2, N_HEADS, HEAD_DIM, HEAD_DIM),
                                     lambda b: (b, layer, 0, 0, 0, 0)))
        args.append(s0)
    out_specs = [pl.BlockSpec((seq_len, W_GROUP), lambda b: (row_block0 + b, 0))]
    out_shape = [jax.ShapeDtypeStruct((N_TOK, W_GROUP), F32)]
    if want_state:
        out_specs.append(pl.BlockSpec((None, 2, N_HEADS, HEAD_DIM, HEAD_DIM),
                                      lambda b: (b, 0, 0, 0, 0)))
        out_shape.append(jax.ShapeDtypeStruct((n_seq, 2, N_HEADS, HEAD_DIM, HEAD_DIM), F32))
    aliases = {}
    if a_prev is not None:
        in_specs.append(pl.BlockSpec(memory_space=pl.ANY))
        args.append(a_prev)
        aliases = {len(args) - 1: 0}
    kern = functools.partial(_hgrn_kernel_aliased if a_prev is not None else _hgrn_kernel,
                             n_chunks=seq_len // HC, has_s0=has_s0, want_state=want_state)
    return pl.pallas_call(
        kern,
        grid=(n_seq,),
        in_specs=in_specs,
        out_specs=out_specs,
        out_shape=out_shape,
        scratch_shapes=[
            pltpu.VMEM((2 * N_HEADS, HEAD_DIM, HEAD_DIM), F32),
            pltpu.VMEM((seq_len, W_GROUP), F32),
            pltpu.VMEM((seq_len, W_GROUP), F32),
        ],
        input_output_aliases=aliases,
        compiler_params=_params("arbitrary"),
        name="hgrn_lat" if has_s0 else "hgrn_ctx",
    )(*args)


def _hgrn_kernel_aliased(*refs, n_chunks, has_s0, want_state):
    n_in = 7 + int(has_s0)
    _hgrn_kernel(*refs[:n_in], *refs[n_in + 1:], n_chunks=n_chunks, has_s0=has_s0,
                 want_state=want_state)


def _mix_kernel(a_ref, bu_ref, bv_ref, cb_ref, cc_ref, ch_ref, dx_ref,
                ccp_ref, chp_ref, dxp_ref, ccn_ref, chn_ref, dxn_ref,
                x_ref, g1_ref, sgun_ref, wsgu_ref, bsgu_ref, wconv_ref, wpool_ref, pscale_ref,
                wout_ref, o_ref, cat_scr):
    i = pl.program_id(0)
    is_ctx = i < N_TMIX_CTX
    lat_tile = jnp.maximum(i - N_TMIX_CTX, 0)
    j = jnp.where(is_ctx, 0, lat_tile % TMIX_PER_LAT)
    tiles_in_seq = jnp.where(is_ctx, 1, TMIX_PER_LAT)
    keep_prev = jnp.where(j == 0, 0.0, 1.0)
    keep_next = jnp.where(j == tiles_in_seq - 1, 0.0, 1.0)
    mod_row = jnp.where(is_ctx, 0, 1 + lat_tile // TMIX_PER_LAT)
    row = lax.broadcasted_iota(jnp.int32, (TMIX, 1), 0)

    cat_scr[:, 0:W_GROUP] = a_ref[...].astype(BF16)

    u = jax.nn.gelu(bu_ref[...])
    vv = jax.nn.gelu(bv_ref[...])
    vn = (vv * lax.rsqrt(jnp.mean(vv * vv, axis=-1, keepdims=True) + EPS) * sgun_ref[...]).astype(BF16)
    for n in range(TMIX // CHUNK_B):
        rs = slice(n * CHUNK_B, (n + 1) * CHUNK_B)
        for h in range(N_HEADS):
            cs = slice(h * HEAD_DIM, (h + 1) * HEAD_DIM)
            mixed = _dot(wsgu_ref[h], vn[rs, cs]) + bsgu_ref[:, cs]
            cat_scr[rs, W_GROUP + h * HEAD_DIM:W_GROUP + (h + 1) * HEAD_DIM] = (u[rs, cs] * mixed).astype(BF16)

    p = cc_ref[...] * ch_ref[...]
    p_edge_prev = ccp_ref[POOL_HALO - 1:POOL_HALO, :] * chp_ref[POOL_HALO - 1:POOL_HALO, :] * keep_prev
    p_edge_next = ccn_ref[0:1, :] * chn_ref[0:1, :] * keep_next
    p_prev = jnp.where(row == 0, p_edge_prev, pltpu.roll(p, 1, 0))
    p_next = jnp.where(row == TMIX - 1, p_edge_next, pltpu.roll(p, TMIX - 1, 0))
    wc = wconv_ref[...]
    c_out = cb_ref[...] * (wc[0:1] * p_prev + wc[1:2] * p + wc[2:3] * p_next)
    cat_scr[:, 2 * W_GROUP:3 * W_GROUP] = c_out.astype(BF16)

    xd = dx_ref[...]
    ext = jnp.concatenate([dxp_ref[...] * keep_prev, xd, dxn_ref[...] * keep_next], axis=0)
    n_ext = TMIX + 2 * POOL_HALO
    back = lambda a, s: pltpu.roll(a, s, 0)
    ahead = lambda a, s: pltpu.roll(a, n_ext - s, 0)
    sums = [back(ext, 1) + ext]
    for s in (1, 2, 4):
        sums.append(back(sums[-1], s) + ahead(sums[-1], s))
    pos = j * TMIX + row
    seq_len = tiles_in_seq * TMIX
    gd = W_GROUP // len(POOL_WINDOWS)
    for g, win in enumerate(POOL_WINDOWS):
        cs = slice(g * gd, (g + 1) * gd)
        cnt = (jnp.minimum(pos + win // 2, seq_len) - jnp.maximum(pos - win // 2, 0)).astype(F32)
        mean = sums[g][POOL_HALO:POOL_HALO + TMIX, cs] / cnt
        dg = _dot((mean - xd[:, cs]).astype(BF16), wpool_ref[g]) * pscale_ref[:, cs]
        cat_scr[:, 3 * W_GROUP + g * gd:3 * W_GROUP + (g + 1) * gd] = dg.astype(BF16)

    y = _dot(cat_scr[...], wout_ref[...])
    o_ref[...] = x_ref[...] + g1_ref[pl.ds(mod_row, 1), :] * y


def _mix(a, z, x, mod, sgu_norm, w_sgu, b_sgu_full, w_conv, w_pool, pool_scale, w_out, layer):
    halo_blocks = TMIX // POOL_HALO
    last_halo = N_TOK // POOL_HALO - 1
    zcol = lambda k: pl.BlockSpec((TMIX, W_GROUP), lambda i: (i, k))
    zprev = lambda k: pl.BlockSpec((POOL_HALO, W_GROUP),
                                   lambda i: (jnp.maximum(i * halo_blocks - 1, 0), k))
    znext = lambda k: pl.BlockSpec((POOL_HALO, W_GROUP),
                                   lambda i: (jnp.minimum((i + 1) * halo_blocks, last_halo), k))
    full = lambda arr: pl.BlockSpec(arr.shape, lambda i, nd=arr.ndim: (0,) * nd)
    of_layer = lambda arr: pl.BlockSpec((None,) + arr.shape[1:],
                                        lambda i, nd=arr.ndim: (layer,) + (0,) * (nd - 1))
    return pl.pallas_call(
        _mix_kernel,
        grid=(N_TOK // TMIX,),
        in_specs=[
            pl.BlockSpec((TMIX, W_GROUP), lambda i: (i, 0)),
            zcol(5), zcol(6), zcol(7), zcol(8), zcol(9), zcol(10),
            zprev(8), zprev(9), zprev(10), znext(8), znext(9), znext(10),
            pl.BlockSpec((TMIX, D_MODEL), lambda i: (i, 0)),
            _mod_spec(layer, 2),
            full(sgu_norm), of_layer(w_sgu), full(b_sgu_full), full(w_conv), of_layer(w_pool),
            full(pool_scale), of_layer(w_out),
        ],
        out_specs=pl.BlockSpec((TMIX, D_MODEL), lambda i: (i, 0)),
        out_shape=jax.ShapeDtypeStruct((N_TOK, D_MODEL), F32),
        scratch_shapes=[pltpu.VMEM((TMIX, D_MODEL), BF16)],
        compiler_params=_params("arbitrary"),
        name="mix",
    )(a, z, z, z, z, z, z, z, z, z, z, z, z, x, mod, sgu_norm, w_sgu, b_sgu_full, w_conv, w_pool,
      pool_scale, w_out)


def _ffn_kernel(x_ref, g_ref, sh_ref, sc_ref, g2_ref, wv_ref, wg_ref, cv_ref, cg_ref, wd_ref,
                o_ref, h_scr, uv_scr, ug_scr, act_scr):
    i = pl.program_id(0)
    j = pl.program_id(1)
    r = _mod_row_of_big_tile(i)

    @pl.when(j == 0)
    def _():
        _norm_mod_rows(x_ref, h_scr, g_ref[...], sh_ref[pl.ds(r, 1), :], sc_ref[pl.ds(r, 1), :], TM)

        def zero_body(n, carry):
            rows = pl.ds(pl.multiple_of(n * NORM_ROWS, NORM_ROWS), NORM_ROWS)
            o_ref[rows, :] = jnp.zeros((NORM_ROWS, D_MODEL), F32)
            return carry
        lax.fori_loop(0, TM // NORM_ROWS, zero_body, 0)

    inner_keep = jnp.where(i < N_TM_CTX, 0.0, 1.0)

    def conv3_rows(u_scr, w, r0, cs):
        cur = u_scr[r0:r0 + FFN_RC, cs]
        if r0 == 0:
            prev = jnp.zeros((POOL_HALO, FFN_CT), F32)
        else:
            prev = u_scr[r0 - POOL_HALO:r0, cs]
            if r0 % SEQ == 0:
                prev = prev * inner_keep
        r1 = r0 + FFN_RC
        if r1 == TM:
            nxt = jnp.zeros((POOL_HALO, FFN_CT), F32)
        else:
            nxt = u_scr[r1:r1 + POOL_HALO, cs]
            if r1 % SEQ == 0:
                nxt = nxt * inner_keep
        ext = jnp.concatenate([prev, cur, nxt], axis=0)
        n_ext = FFN_RC + 2 * POOL_HALO
        before = pltpu.roll(ext, 1, 0)[POOL_HALO:POOL_HALO + FFN_RC]
        after = pltpu.roll(ext, n_ext - 1, 0)[POOL_HALO:POOL_HALO + FFN_RC]
        return w[0:1] * before + w[1:2] * cur + w[2:3] * after

    col_tiles = [slice(c * FFN_CT, (c + 1) * FFN_CT) for c in range(TF // FFN_CT)]
    for cs in col_tiles:
        uv_scr[:, cs] = _dot(h_scr[...], wv_ref[:, cs])
        ug_scr[:, cs] = _dot(h_scr[...], wg_ref[:, cs])
    for cs in col_tiles:
        cv = cv_ref[:, cs]
        cg = cg_ref[:, cs]
        for r0 in range(0, TM, FFN_RC):
            val = conv3_rows(uv_scr, cv, r0, cs)
            gate = conv3_rows(ug_scr, cg, r0, cs)
            act_scr[r0:r0 + FFN_RC, cs] = (val * _silu(gate)).astype(BF16)
    for cs in col_tiles:
        for c in range(D_MODEL // FFN_NOUT):
            ns = slice(c * FFN_NOUT, (c + 1) * FFN_NOUT)
            o_ref[:, ns] += _dot(act_scr[:, cs], wd_ref[cs, ns])

    @pl.when(j == pl.num_programs(1) - 1)
    def _():
        g2 = g2_ref[pl.ds(r, 1), :]

        def body(n, carry):
            rows = pl.ds(pl.multiple_of(n * NORM_ROWS, NORM_ROWS), NORM_ROWS)
            o_ref[rows, :] = x_ref[rows, :] + g2 * o_ref[rows, :]
            return carry
        lax.fori_loop(0, TM // NORM_ROWS, body, 0)


def _ffn(x, gain, mod, w_up, w_conv, w_down, layer):
    n_f = D_FF // TF
    return pl.pallas_call(
        _ffn_kernel,
        grid=(N_TM, n_f),
        in_specs=[
            pl.BlockSpec((TM, D_MODEL), lambda i, j: (i, 0), pipeline_mode=pl.Buffered(1)),
            pl.BlockSpec((1, D_MODEL), lambda i, j: (0, 0)),
            _mod_spec(layer, 3),
            _mod_spec(layer, 4),
            _mod_spec(layer, 5),
            pl.BlockSpec((None, D_MODEL, TF), lambda i, j: (layer, 0, j)),
            pl.BlockSpec((None, D_MODEL, TF), lambda i, j: (layer, 0, j + n_f)),
            pl.BlockSpec((None, 3, TF), lambda i, j: (layer, 0, j)),
            pl.BlockSpec((None, 3, TF), lambda i, j: (layer, 0, j + n_f)),
            pl.BlockSpec((None, TF, D_MODEL), lambda i, j: (layer, j, 0)),
        ],
        out_specs=pl.BlockSpec((TM, D_MODEL), lambda i, j: (i, 0)),
        out_shape=jax.ShapeDtypeStruct((N_TOK, D_MODEL), F32),
        scratch_shapes=[pltpu.VMEM((TM, D_MODEL), BF16), pltpu.VMEM((TM, TF), F32),
                        pltpu.VMEM((TM, TF), F32), pltpu.VMEM((TM, TF), BF16)],
        compiler_params=_params("arbitrary", "arbitrary"),
        name="ffn",
    )(x, gain, mod, mod, mod, w_up, w_up, w_conv, w_conv, w_down)


def _final_norm_kernel(x_ref, g_ref, o_ref):
    gain = g_ref[...]

    def body(n, carry):
        rows = pl.ds(pl.multiple_of(n * NORM_ROWS, NORM_ROWS), NORM_ROWS)
        x = x_ref[rows, :]
        o_ref[rows, :] = x * lax.rsqrt(jnp.mean(x * x, axis=-1, keepdims=True) + EPS) * gain
        return carry
    lax.fori_loop(0, TM // NORM_ROWS, body, 0)


def _final_norm(x, gain, tile0, n_tiles):
    return pl.pallas_call(
        _final_norm_kernel,
        grid=(n_tiles,),
        in_specs=[pl.BlockSpec((TM, D_MODEL), lambda i: (tile0 + i, 0)),
                  pl.BlockSpec((1, D_MODEL), lambda i: (0, 0))],
        out_specs=pl.BlockSpec((TM, D_MODEL), lambda i: (i, 0)),
        out_shape=jax.ShapeDtypeStruct((n_tiles * TM, D_MODEL), F32),
        compiler_params=_params("arbitrary"),
        name="final_norm",
    )(x, gain)


def _grid_pos_embed():
    rows = DEC_SEQ // GRID_W
    quarter = D_MODEL // 4
    freq = jnp.exp(-jnp.log(10000.0) * jnp.arange(quarter, dtype=F32) / quarter)[None, :]
    r = jnp.arange(rows, dtype=F32)[:, None] * freq
    col = jnp.arange(GRID_W, dtype=F32)[:, None] * freq
    rep = lambda a: jnp.repeat(a, GRID_W, axis=0)
    til = lambda a: jnp.tile(a, (rows, 1))
    return jnp.concatenate([rep(jnp.sin(r)), rep(jnp.cos(r)), til(jnp.sin(col)), til(jnp.cos(col))], -1)


def kernel(x_prompt, x_sample, c, state_hgrn, c_ctx, w_ada, b_ada, norm_mix, norm_ffn, w_in,
           lb_logits, hgrn_norm, sgu_norm, w_sgu, b_sgu, w_conv_c, w_pool, pool_scale, w_out,
           w_up, w_conv_ffn, w_down, norm_final):
    p = jax.nn.softmax(lb_logits.astype(F32), axis=0)
    lbs = jnp.cumsum(p, axis=0) - p[0:1]

    x = jnp.concatenate([x_prompt.reshape(N_CTX, D_MODEL),
                         (x_sample + _grid_pos_embed()[None]).reshape(N_LAT, D_MODEL)], axis=0)
    cvec = jnp.concatenate([c_ctx[None, :], c, jnp.zeros((MOD_ROWS - 1 - DEC_BATCH, D_MODEL), F32)], 0)
    mod = _adaln(cvec, w_ada, b_ada)

    consts = _hgrn_constants()
    w_sgu, w_pool, w_out, w_up, w_down = (
        w.astype(BF16) for w in (w_sgu, w_pool, w_out, w_up, w_down))
    states = []
    for l in range(DEPTH):
        row = lambda a: a[l].reshape(1, -1)
        z = _inproj(x, row(norm_mix), mod, w_in, l)
        a_ctx, st = _hgrn(z, lbs[l], row(hgrn_norm), consts, seq_len=SEQ, n_seq=BATCH, row_block0=0)
        a = _hgrn(z, lbs[l], row(hgrn_norm), consts, seq_len=DEC_SEQ, n_seq=DEC_BATCH,
                  row_block0=N_CTX // DEC_SEQ, s0=state_hgrn, layer=l, a_prev=a_ctx)[0]
        b_sgu_full = jnp.repeat(b_sgu[l].T, HEAD_DIM, axis=1)
        x = _mix(a, z, x, mod, row(sgu_norm), w_sgu, b_sgu_full, w_conv_c[l], w_pool,
                 row(pool_scale), w_out, l)
        x = _ffn(x, row(norm_ffn), mod, w_up, w_conv_ffn, w_down, l)
        states.append(st)

    gain = norm_final.reshape(1, -1)
    y_prompt = _final_norm(x, gain, 0, N_TM_CTX).reshape(BATCH, SEQ, D_MODEL)
    y_sample = _final_norm(x, gain, N_TM_CTX, N_TM - N_TM_CTX).reshape(DEC_BATCH, DEC_SEQ, D_MODEL)
    return (y_prompt, y_sample, jnp.stack(states, axis=1))
```

```python
import functools

import numpy as np
import jax
import jax.numpy as jnp
from jax import lax
from jax.experimental import pallas as pl
from jax.experimental.pallas import tpu as pltpu

F32 = jnp.float32
BF16 = jnp.bfloat16

D_MODEL = 2048
BATCH, SEQ = 16, 256
DEC_BATCH, DEC_SEQ = 2, 1024
DEPTH = 2
GRID_W = 64
HEAD_DIM = 128
W_GROUP = D_MODEL // 4
N_HEADS = W_GROUP // HEAD_DIM
POOL_WINDOWS = (2, 4, 8, 16)
POOL_HALO = 8
CHUNK_B = 128
D_FF = 5632
D_IN = 11 * W_GROUP
EPS = 1e-6

N_CTX = BATCH * SEQ
N_LAT = DEC_BATCH * DEC_SEQ
N_TOK = N_CTX + N_LAT
MOD_ROWS = 8

TM = 1024
N_TM = N_TOK // TM
N_TM_CTX = N_CTX // TM
TM_IN = 2048
TN_IN = 512
TF = 512
FFN_CT = 256
FFN_RC = 64
FFN_NOUT = 512
TMIX = 256
N_TMIX_CTX = N_CTX // TMIX
TMIX_PER_LAT = DEC_SEQ // TMIX
TP = 512
N_TP_CTX = N_CTX // TP
TP_PER_LAT = DEC_SEQ // TP
TN_ADA = 1024
NORM_ROWS = 128
CAST_ROWS = 32

HC = 64
HC_LEVELS = 6
VMEM_LIMIT = 56 * 1024 * 1024


def _silu(x):
    return x * jax.nn.sigmoid(x)


def _dot(a, b):
    return jnp.dot(a, b, preferred_element_type=F32)


def _dot_nt(a, b):
    return lax.dot_general(a, b, (((1,), (1,)), ((), ())), preferred_element_type=F32)


def _params(*sem):
    return pltpu.CompilerParams(dimension_semantics=sem, vmem_limit_bytes=VMEM_LIMIT)


def _adaln_kernel(c_ref, w_ref, b_ref, o_ref):
    s = _silu(c_ref[...]).astype(BF16)
    o_ref[...] = _dot(s, w_ref[...].astype(BF16)) + b_ref[...]


def _adaln(cvec, w_ada, b_ada):
    n_out = w_ada.shape[-1]
    return pl.pallas_call(
        _adaln_kernel,
        grid=(DEPTH, n_out // TN_ADA),
        in_specs=[
            pl.BlockSpec((MOD_ROWS, D_MODEL), lambda l, j: (0, 0)),
            pl.BlockSpec((None, D_MODEL, TN_ADA), lambda l, j: (l, 0, j)),
            pl.BlockSpec((None, 1, TN_ADA), lambda l, j: (l, 0, j)),
        ],
        out_specs=pl.BlockSpec((None, MOD_ROWS, TN_ADA), lambda l, j: (l, 0, j)),
        out_shape=jax.ShapeDtypeStruct((DEPTH, MOD_ROWS, n_out), F32),
        compiler_params=_params("arbitrary", "arbitrary"),
        name="adaln",
    )(cvec, w_ada, b_ada.reshape(DEPTH, 1, n_out))


def _norm_mod_rows(x_ref, h_ref, gain, shift, scale, n_rows):
    gain_eff = gain * (1.0 + scale)

    def body(n, carry):
        r0 = pl.multiple_of(n * NORM_ROWS, NORM_ROWS)
        x = x_ref[pl.ds(r0, NORM_ROWS), :]
        ms = jnp.mean(x * x, axis=-1, keepdims=True)
        h_ref[pl.ds(r0, NORM_ROWS), :] = (x * lax.rsqrt(ms + EPS) * gain_eff + shift).astype(BF16)
        return carry
    lax.fori_loop(0, n_rows // NORM_ROWS, body, 0)


def _mod_row_of_big_tile(i):
    return jnp.maximum(i - (N_TM_CTX - 1), 0)


def _mod_spec(layer, k):
    return pl.BlockSpec((None, MOD_ROWS, D_MODEL), lambda *_: (layer, 0, k))


def _prep_kernel(xp_ref, xs_ref, pos_ref, g_ref, sh_ref, sc_ref, x_ref, h_ref):
    i = pl.program_id(0)
    lat_tile = jnp.maximum(i - N_TP_CTX, 0)
    r = jnp.where(i < N_TP_CTX, 0, 1 + lat_tile // TP_PER_LAT)

    @pl.when(i < N_TP_CTX)
    def _():
        def body(n, carry):
            rows = pl.ds(pl.multiple_of(n * NORM_ROWS, NORM_ROWS), NORM_ROWS)
            x_ref[rows, :] = xp_ref[rows, :]
            return carry
        lax.fori_loop(0, TP // NORM_ROWS, body, 0)

    @pl.when(i >= N_TP_CTX)
    def _():
        def body(n, carry):
            rows = pl.ds(pl.multiple_of(n * NORM_ROWS, NORM_ROWS), NORM_ROWS)
            x_ref[rows, :] = xs_ref[rows, :] + pos_ref[rows, :]
            return carry
        lax.fori_loop(0, TP // NORM_ROWS, body, 0)

    _norm_mod_rows(x_ref, h_ref, g_ref[...], sh_ref[pl.ds(r, 1), :], sc_ref[pl.ds(r, 1), :], TP)


def _prep(x_prompt, x_sample, pos, gain, mod, layer):
    lat = lambda i: jnp.maximum(i - N_TP_CTX, 0)
    return pl.pallas_call(
        _prep_kernel,
        grid=(N_TOK // TP,),
        in_specs=[
            pl.BlockSpec((TP, D_MODEL), lambda i: (jnp.minimum(i, N_TP_CTX - 1), 0)),
            pl.BlockSpec((TP, D_MODEL), lambda i: (lat(i), 0)),
            pl.BlockSpec((TP, D_MODEL), lambda i: (lat(i) % TP_PER_LAT, 0)),
            pl.BlockSpec((1, D_MODEL), lambda i: (0, 0)),
            _mod_spec(layer, 0),
            _mod_spec(layer, 1),
        ],
        out_specs=[pl.BlockSpec((TP, D_MODEL), lambda i: (i, 0)),
                   pl.BlockSpec((TP, D_MODEL), lambda i: (i, 0))],
        out_shape=[jax.ShapeDtypeStruct((N_TOK, D_MODEL), F32),
                   jax.ShapeDtypeStruct((N_TOK, D_MODEL), BF16)],
        compiler_params=_params("arbitrary"),
        name="prep",
    )(x_prompt, x_sample, pos, gain, mod, mod)


def _norm_mod_kernel(x_ref, g_ref, sh_ref, sc_ref, h_ref):
    r = _mod_row_of_big_tile(pl.program_id(0))
    _norm_mod_rows(x_ref, h_ref, g_ref[...], sh_ref[pl.ds(r, 1), :], sc_ref[pl.ds(r, 1), :], TM)


def _norm_mod(x, gain, mod, layer):
    return pl.pallas_call(
        _norm_mod_kernel,
        grid=(N_TM,),
        in_specs=[
            pl.BlockSpec((TM, D_MODEL), lambda i: (i, 0)),
            pl.BlockSpec((1, D_MODEL), lambda i: (0, 0)),
            _mod_spec(layer, 0),
            _mod_spec(layer, 1),
        ],
        out_specs=pl.BlockSpec((TM, D_MODEL), lambda i: (i, 0)),
        out_shape=jax.ShapeDtypeStruct((N_TOK, D_MODEL), BF16),
        compiler_params=_params("arbitrary"),
        name="norm_mod",
    )(x, gain, mod, mod)


def _inproj_kernel(h_ref, w_ref, z_ref):
    z_ref[...] = _dot(h_ref[...], w_ref[...].astype(BF16))


def _inproj(h, w_in, layer):
    return pl.pallas_call(
        _inproj_kernel,
        grid=(N_TOK // TM_IN, D_IN // TN_IN),
        in_specs=[
            pl.BlockSpec((TM_IN, D_MODEL), lambda i, j: (i, 0)),
            pl.BlockSpec((None, D_MODEL, TN_IN), lambda i, j: (layer, 0, j)),
        ],
        out_specs=pl.BlockSpec((TM_IN, TN_IN), lambda i, j: (i, j)),
        out_shape=jax.ShapeDtypeStruct((N_TOK, D_IN), F32),
        compiler_params=_params("arbitrary", "arbitrary"),
        name="inproj",
    )(h, w_in)


def _hgrn_constants():
    t = np.arange(HC)
    run = [(t[None, :] <= t[:, None]), (t[None, :] >= t[:, None])]
    g = np.zeros((2, HC_LEVELS + 2, HC, HC), np.float32)
    uq = np.ones((2, HC_LEVELS + 1, HC), np.float32)
    bm = np.zeros((HC_LEVELS + 1, HC, HC), np.float32)
    for d in range(2):
        g[d, 0] = run[d]
        for lev in range(HC_LEVELS):
            m = 1 << lev
            base = (t // (2 * m)) * (2 * m)
            g[d, lev + 1] = run[d][base + m - 1 + d]
            upper = (t % (2 * m)) >= m
            uq[d, lev] = upper if d == 0 else ~upper
        g[d, HC_LEVELS + 1] = run[d][np.full(HC, HC - 1 if d == 0 else 0)]
    uk = 1.0 - uq
    uk[:, HC_LEVELS] = 1.0
    for lev in range(HC_LEVELS):
        m = 1 << lev
        bm[lev] = (t[:, None] // (2 * m)) == (t[None, :] // (2 * m))
    bm[HC_LEVELS] = np.eye(HC)
    g = g.reshape(2, (HC_LEVELS + 2) * HC, HC)
    g3 = np.concatenate([g, g, g], axis=-1)
    lanes = lambda a: np.ascontiguousarray(np.broadcast_to(a[..., None], a.shape + (HEAD_DIM,)))
    return (jnp.asarray(g3, BF16), jnp.asarray(lanes(uq), F32), jnp.asarray(lanes(uk), F32),
            jnp.asarray(bm, F32))


def _hgrn_chunk(d, h, r0, z_ref, lb_ref, g3_ref, uq_ref, uk_ref, bm_ref, s_scr, o_scr):
    c0 = h * HEAD_DIM
    rows = pl.ds(r0, HC)
    q = z_ref[rows, c0:c0 + HEAD_DIM]
    v = z_ref[rows, W_GROUP + c0:W_GROUP + c0 + HEAD_DIM]
    zz = z_ref[rows, (2 + d) * W_GROUP + c0:(2 + d) * W_GROUP + c0 + HEAD_DIM]
    lb = lb_ref[d:d + 1, c0:c0 + HEAD_DIM]
    f = lb + (1.0 - lb) * jax.nn.sigmoid(zz)
    kk = 1.0 - f
    lf = jnp.log(f)

    l1 = lf.astype(BF16)
    r1 = lf - l1.astype(F32)
    l2 = r1.astype(BF16)
    l3 = (r1 - l2.astype(F32)).astype(BF16)
    sums = _dot(g3_ref[d], jnp.concatenate([l1, l2, l3], axis=0))
    cum = sums[0:HC]
    far = sums[(HC_LEVELS + 1) * HC:(HC_LEVELS + 2) * HC]

    vb = v.astype(BF16)
    att = None
    for lev in range(HC_LEVELS + 1):
        if lev < HC_LEVELS:
            ex = jnp.exp(-jnp.abs(cum - sums[(lev + 1) * HC:(lev + 2) * HC]))
            qs = q * ex * uq_ref[d, lev]
            ks = kk * ex * uk_ref[d, lev]
        else:
            qs, ks = q, kk
        sc = _dot_nt(qs.astype(BF16), ks.astype(BF16)) * bm_ref[lev]
        att = sc if att is None else att + sc

    s_old = s_scr[d * N_HEADS + h]
    q_in = (q * jnp.exp(cum)).astype(BF16)
    o_scr[rows, c0:c0 + HEAD_DIM] = _dot(att.astype(BF16), vb) + _dot(q_in, s_old.astype(BF16))

    k_out = kk * jnp.exp(-jnp.abs(cum - far))
    k_ext = jnp.concatenate([k_out, jnp.exp(far[0:8])], axis=0)
    k_ext_t = k_ext.T
    s_scr[d * N_HEADS + h] = k_ext_t[:, HC:HC + 1] * s_old + _dot(k_ext_t[:, 0:HC].astype(BF16), vb)


def _hgrn_kernel(*refs, n_chunks, has_s0, n_cast):
    z_ref, lb_ref, ng_ref, g3_ref, uq_ref, uk_ref, bm_ref = refs[:7]
    k = 7
    s0_ref = st_ref = None
    if has_s0:
        s0_ref = refs[k]
        k += 1
    cast_src = refs[k:k + n_cast]
    k += n_cast
    a_ref = refs[k]
    k += 1
    if not has_s0:
        st_ref = refs[k]
        k += 1
    cast_dst = refs[k:k + n_cast]
    k += n_cast
    s_scr, of_scr, ob_scr = refs[k:k + 3]

    for src, dst in zip(cast_src, cast_dst):
        for r0 in range(0, src.shape[0], CAST_ROWS):
            dst[r0:r0 + CAST_ROWS, :] = src[r0:r0 + CAST_ROWS, :].astype(BF16)

    for d in range(2):
        for h in range(N_HEADS):
            s_scr[d * N_HEADS + h] = (s0_ref[d, h] if has_s0
                                      else jnp.zeros((HEAD_DIM, HEAD_DIM), F32))

    def scan_body(n, carry):
        for d, o_scr in ((0, of_scr), (1, ob_scr)):
            r0 = pl.multiple_of((n if d == 0 else n_chunks - 1 - n) * HC, HC)
            for h in range(N_HEADS):
                _hgrn_chunk(d, h, r0, z_ref, lb_ref, g3_ref, uq_ref, uk_ref, bm_ref, s_scr, o_scr)
        return carry
    lax.fori_loop(0, n_chunks, scan_body, 0)

    def out_body(n, carry):
        rows = pl.ds(pl.multiple_of(n * HC, HC), HC)
        o = of_scr[rows, :] + ob_scr[rows, :]
        parts = []
        for h in range(N_HEADS):
            oh = o[:, h * HEAD_DIM:(h + 1) * HEAD_DIM]
            parts.append(oh * lax.rsqrt(jnp.mean(oh * oh, axis=-1, keepdims=True) + EPS))
        gate = z_ref[rows, 4 * W_GROUP:5 * W_GROUP]
        a_ref[rows, :] = jnp.concatenate(parts, axis=-1) * ng_ref[...] * _silu(gate)
        return carry
    lax.fori_loop(0, n_chunks, out_body, 0)

    if st_ref is not None:
        for d in range(2):
            for h in range(N_HEADS):
                st_ref[d, h] = s_scr[d * N_HEADS + h]


def _hgrn(z, lb, norm_g, consts, *, seq_len, n_seq, row_block0, s0=None, layer=0, casts=()):
    has_s0 = s0 is not None
    const_specs = [pl.BlockSpec(c.shape, lambda b, nd=c.ndim: (0,) * nd) for c in consts]
    in_specs = [
        pl.BlockSpec((seq_len, 5 * W_GROUP), lambda b: (row_block0 + b, 0)),
        pl.BlockSpec((2, W_GROUP), lambda b: (0, 0)),
        pl.BlockSpec((1, W_GROUP), lambda b: (0, 0)),
    ] + const_specs
    args = [z, lb, norm_g, *consts]
    if has_s0:
        in_specs.append(pl.BlockSpec((None, None, 2, N_HEADS, HEAD_DIM, HEAD_DIM),
                                     lambda b: (b, layer, 0, 0, 0, 0)))
        args.append(s0)
    out_specs = [pl.BlockSpec((seq_len, W_GROUP), lambda b: (b, 0))]
    out_shape = [jax.ShapeDtypeStruct((n_seq * seq_len, W_GROUP), F32)]
    if not has_s0:
        out_specs.append(pl.BlockSpec((None, 2, N_HEADS, HEAD_DIM, HEAD_DIM),
                                      lambda b: (b, 0, 0, 0, 0)))
        out_shape.append(jax.ShapeDtypeStruct((n_seq, 2, N_HEADS, HEAD_DIM, HEAD_DIM), F32))
    for w in casts:
        slab, cols = w.shape[1] // n_seq, w.shape[2]
        assert slab * n_seq == w.shape[1] and slab % CAST_ROWS == 0
        in_specs.append(pl.BlockSpec((None, slab, cols), lambda b: (layer, b, 0)))
        args.append(w)
        out_specs.append(pl.BlockSpec((slab, cols), lambda b: (b, 0)))
        out_shape.append(jax.ShapeDtypeStruct(w.shape[1:], BF16))
    kern = functools.partial(_hgrn_kernel, n_chunks=seq_len // HC, has_s0=has_s0,
                             n_cast=len(casts))
    return pl.pallas_call(
        kern,
        grid=(n_seq,),
        in_specs=in_specs,
        out_specs=out_specs,
        out_shape=out_shape,
        scratch_shapes=[
            pltpu.VMEM((2 * N_HEADS, HEAD_DIM, HEAD_DIM), F32),
            pltpu.VMEM((seq_len, W_GROUP), F32),
            pltpu.VMEM((seq_len, W_GROUP), F32),
        ],
        compiler_params=_params("arbitrary"),
        name="hgrn_lat" if has_s0 else "hgrn_ctx",
    )(*args)


def _mix_kernel(actx_ref, alat_ref, bu_ref, bv_ref, cb_ref, cc_ref, ch_ref, dx_ref,
                ccp_ref, chp_ref, dxp_ref, ccn_ref, chn_ref, dxn_ref,
                x_ref, g1_ref, sgun_ref, wsgu_ref, bsgu_ref, wconv_ref, wpool_ref, pscale_ref,
                wout_ref, o_ref, cat_scr):
    i = pl.program_id(0)
    is_ctx = i < N_TMIX_CTX
    lat_tile = jnp.maximum(i - N_TMIX_CTX, 0)
    j = jnp.where(is_ctx, 0, lat_tile % TMIX_PER_LAT)
    tiles_in_seq = jnp.where(is_ctx, 1, TMIX_PER_LAT)
    keep_prev = jnp.where(j == 0, 0.0, 1.0)
    keep_next = jnp.where(j == tiles_in_seq - 1, 0.0, 1.0)
    mod_row = jnp.where(is_ctx, 0, 1 + lat_tile // TMIX_PER_LAT)
    row = lax.broadcasted_iota(jnp.int32, (TMIX, 1), 0)

    cat_scr[:, 0:W_GROUP] = jnp.where(is_ctx, actx_ref[...], alat_ref[...]).astype(BF16)

    u = jax.nn.gelu(bu_ref[...])
    vv = jax.nn.gelu(bv_ref[...])
    vn = (vv * lax.rsqrt(jnp.mean(vv * vv, axis=-1, keepdims=True) + EPS) * sgun_ref[...]).astype(BF16)
    for n in range(TMIX // CHUNK_B):
        rs = slice(n * CHUNK_B, (n + 1) * CHUNK_B)
        for h in range(N_HEADS):
            cs = slice(h * HEAD_DIM, (h + 1) * HEAD_DIM)
            mixed = _dot(wsgu_ref[h], vn[rs, cs]) + bsgu_ref[:, cs]
            cat_scr[rs, W_GROUP + h * HEAD_DIM:W_GROUP + (h + 1) * HEAD_DIM] = (u[rs, cs] * mixed).astype(BF16)

    p = cc_ref[...] * ch_ref[...]
    p_edge_prev = ccp_ref[POOL_HALO - 1:POOL_HALO, :] * chp_ref[POOL_HALO - 1:POOL_HALO, :] * keep_prev
    p_edge_next = ccn_ref[0:1, :] * chn_ref[0:1, :] * keep_next
    p_prev = jnp.where(row == 0, p_edge_prev, pltpu.roll(p, 1, 0))
    p_next = jnp.where(row == TMIX - 1, p_edge_next, pltpu.roll(p, TMIX - 1, 0))
    wc = wconv_ref[...]
    c_out = cb_ref[...] * (wc[0:1] * p_prev + wc[1:2] * p + wc[2:3] * p_next)
    cat_scr[:, 2 * W_GROUP:3 * W_GROUP] = c_out.astype(BF16)

    xd = dx_ref[...]
    ext = jnp.concatenate([dxp_ref[...] * keep_prev, xd, dxn_ref[...] * keep_next], axis=0)
    n_ext = TMIX + 2 * POOL_HALO
    back = lambda a, s: pltpu.roll(a, s, 0)
    ahead = lambda a, s: pltpu.roll(a, n_ext - s, 0)
    sums = [back(ext, 1) + ext]
    for s in (1, 2, 4):
        sums.append(back(sums[-1], s) + ahead(sums[-1], s))
    pos = j * TMIX + row
    seq_len = tiles_in_seq * TMIX
    gd = W_GROUP // len(POOL_WINDOWS)
    for g, win in enumerate(POOL_WINDOWS):
        cs = slice(g * gd, (g + 1) * gd)
        cnt = (jnp.minimum(pos + win // 2, seq_len) - jnp.maximum(pos - win // 2, 0)).astype(F32)
        mean = sums[g][POOL_HALO:POOL_HALO + TMIX, cs] / cnt
        dg = _dot((mean - xd[:, cs]).astype(BF16), wpool_ref[g]) * pscale_ref[:, cs]
        cat_scr[:, 3 * W_GROUP + g * gd:3 * W_GROUP + (g + 1) * gd] = dg.astype(BF16)

    y = _dot(cat_scr[...], wout_ref[...])
    o_ref[...] = x_ref[...] + g1_ref[pl.ds(mod_row, 1), :] * y


def _mix(a_ctx, a_lat, z, x, mod, sgu_norm, w_sgu, b_sgu_full, w_conv, w_pool, pool_scale, w_out,
         layer):
    halo_blocks = TMIX // POOL_HALO
    last_halo = N_TOK // POOL_HALO - 1
    zcol = lambda k: pl.BlockSpec((TMIX, W_GROUP), lambda i: (i, k))
    zprev = lambda k: pl.BlockSpec((POOL_HALO, W_GROUP),
                                   lambda i: (jnp.maximum(i * halo_blocks - 1, 0), k))
    znext = lambda k: pl.BlockSpec((POOL_HALO, W_GROUP),
                                   lambda i: (jnp.minimum((i + 1) * halo_blocks, last_halo), k))
    full = lambda arr: pl.BlockSpec(arr.shape, lambda i, nd=arr.ndim: (0,) * nd)
    of_layer = lambda arr: pl.BlockSpec((None,) + arr.shape[1:],
                                        lambda i, nd=arr.ndim: (layer,) + (0,) * (nd - 1))
    return pl.pallas_call(
        _mix_kernel,
        grid=(N_TOK // TMIX,),
        in_specs=[
            pl.BlockSpec((TMIX, W_GROUP), lambda i: (jnp.minimum(i, N_TMIX_CTX - 1), 0)),
            pl.BlockSpec((TMIX, W_GROUP), lambda i: (jnp.maximum(i - N_TMIX_CTX, 0), 0)),
            zcol(5), zcol(6), zcol(7), zcol(8), zcol(9), zcol(10),
            zprev(8), zprev(9), zprev(10), znext(8), znext(9), znext(10),
            pl.BlockSpec((TMIX, D_MODEL), lambda i: (i, 0)),
            _mod_spec(layer, 2),
            full(sgu_norm), of_layer(w_sgu), full(b_sgu_full), full(w_conv), of_layer(w_pool),
            full(pool_scale), full(w_out),
        ],
        out_specs=pl.BlockSpec((TMIX, D_MODEL), lambda i: (i, 0)),
        out_shape=jax.ShapeDtypeStruct((N_TOK, D_MODEL), F32),
        scratch_shapes=[pltpu.VMEM((TMIX, D_MODEL), BF16)],
        compiler_params=_params("arbitrary"),
        name="mix",
    )(a_ctx, a_lat, z, z, z, z, z, z, z, z, z, z, z, z, x, mod, sgu_norm, w_sgu, b_sgu_full, w_conv, w_pool,
      pool_scale, w_out)


def _ffn_kernel(x_ref, g_ref, sh_ref, sc_ref, g2_ref, wv_ref, wg_ref, cv_ref, cg_ref, wd_ref,
                o_ref, h_scr, uv_scr, ug_scr, act_scr):
    i = pl.program_id(0)
    j = pl.program_id(1)
    r = _mod_row_of_big_tile(i)

    @pl.when(j == 0)
    def _():
        _norm_mod_rows(x_ref, h_scr, g_ref[...], sh_ref[pl.ds(r, 1), :], sc_ref[pl.ds(r, 1), :], TM)

        def zero_body(n, carry):
            rows = pl.ds(pl.multiple_of(n * NORM_ROWS, NORM_ROWS), NORM_ROWS)
            o_ref[rows, :] = jnp.zeros((NORM_ROWS, D_MODEL), F32)
            return carry
        lax.fori_loop(0, TM // NORM_ROWS, zero_body, 0)

    inner_keep = jnp.where(i < N_TM_CTX, 0.0, 1.0)

    def conv3_rows(u_scr, w, r0, cs):
        cur = u_scr[r0:r0 + FFN_RC, cs]
        if r0 == 0:
            prev = jnp.zeros((POOL_HALO, FFN_CT), F32)
        else:
            prev = u_scr[r0 - POOL_HALO:r0, cs]
            if r0 % SEQ == 0:
                prev = prev * inner_keep
        r1 = r0 + FFN_RC
        if r1 == TM:
            nxt = jnp.zeros((POOL_HALO, FFN_CT), F32)
        else:
            nxt = u_scr[r1:r1 + POOL_HALO, cs]
            if r1 % SEQ == 0:
                nxt = nxt * inner_keep
        ext = jnp.concatenate([prev, cur, nxt], axis=0)
        n_ext = FFN_RC + 2 * POOL_HALO
        before = pltpu.roll(ext, 1, 0)[POOL_HALO:POOL_HALO + FFN_RC]
        after = pltpu.roll(ext, n_ext - 1, 0)[POOL_HALO:POOL_HALO + FFN_RC]
        return w[0:1] * before + w[1:2] * cur + w[2:3] * after

    col_tiles = [slice(c * FFN_CT, (c + 1) * FFN_CT) for c in range(TF // FFN_CT)]
    for cs in col_tiles:
        uv_scr[:, cs] = _dot(h_scr[...], wv_ref[:, cs])
        ug_scr[:, cs] = _dot(h_scr[...], wg_ref[:, cs])
    for cs in col_tiles:
        cv = cv_ref[:, cs]
        cg = cg_ref[:, cs]
        for r0 in range(0, TM, FFN_RC):
            val = conv3_rows(uv_scr, cv, r0, cs)
            gate = conv3_rows(ug_scr, cg, r0, cs)
            act_scr[r0:r0 + FFN_RC, cs] = (val * _silu(gate)).astype(BF16)
    for cs in col_tiles:
        for c in range(D_MODEL // FFN_NOUT):
            ns = slice(c * FFN_NOUT, (c + 1) * FFN_NOUT)
            o_ref[:, ns] += _dot(act_scr[:, cs], wd_ref[cs, ns])

    @pl.when(j == pl.num_programs(1) - 1)
    def _():
        g2 = g2_ref[pl.ds(r, 1), :]

        def body(n, carry):
            rows = pl.ds(pl.multiple_of(n * NORM_ROWS, NORM_ROWS), NORM_ROWS)
            o_ref[rows, :] = x_ref[rows, :] + g2 * o_ref[rows, :]
            return carry
        lax.fori_loop(0, TM // NORM_ROWS, body, 0)


def _ffn(x, gain, mod, w_up, w_conv, w_down, layer):
    n_f = D_FF // TF
    return pl.pallas_call(
        _ffn_kernel,
        grid=(N_TM, n_f),
        in_specs=[
            pl.BlockSpec((TM, D_MODEL), lambda i, j: (i, 0), pipeline_mode=pl.Buffered(1)),
            pl.BlockSpec((1, D_MODEL), lambda i, j: (0, 0)),
            _mod_spec(layer, 3),
            _mod_spec(layer, 4),
            _mod_spec(layer, 5),
            pl.BlockSpec((D_MODEL, TF), lambda i, j: (0, j)),
            pl.BlockSpec((D_MODEL, TF), lambda i, j: (0, j + n_f)),
            pl.BlockSpec((None, 3, TF), lambda i, j: (layer, 0, j)),
            pl.BlockSpec((None, 3, TF), lambda i, j: (layer, 0, j + n_f)),
            pl.BlockSpec((TF, D_MODEL), lambda i, j: (j, 0)),
        ],
        out_specs=pl.BlockSpec((TM, D_MODEL), lambda i, j: (i, 0)),
        out_shape=jax.ShapeDtypeStruct((N_TOK, D_MODEL), F32),
        scratch_shapes=[pltpu.VMEM((TM, D_MODEL), BF16), pltpu.VMEM((TM, TF), F32),
                        pltpu.VMEM((TM, TF), F32), pltpu.VMEM((TM, TF), BF16)],
        compiler_params=_params("arbitrary", "arbitrary"),
        name="ffn",
    )(x, gain, mod, mod, mod, w_up, w_up, w_conv, w_conv, w_down)


def _final_norm_kernel(x_ref, g_ref, o_ref):
    gain = g_ref[...]

    def body(n, carry):
        rows = pl.ds(pl.multiple_of(n * NORM_ROWS, NORM_ROWS), NORM_ROWS)
        x = x_ref[rows, :]
        o_ref[rows, :] = x * lax.rsqrt(jnp.mean(x * x, axis=-1, keepdims=True) + EPS) * gain
        return carry
    lax.fori_loop(0, TM // NORM_ROWS, body, 0)


def _final_norm(x, gain, tile0, n_tiles):
    return pl.pallas_call(
        _final_norm_kernel,
        grid=(n_tiles,),
        in_specs=[pl.BlockSpec((TM, D_MODEL), lambda i: (tile0 + i, 0)),
                  pl.BlockSpec((1, D_MODEL), lambda i: (0, 0))],
        out_specs=pl.BlockSpec((TM, D_MODEL), lambda i: (i, 0)),
        out_shape=jax.ShapeDtypeStruct((n_tiles * TM, D_MODEL), F32),
        compiler_params=_params("arbitrary"),
        name="final_norm",
    )(x, gain)


def _grid_pos_embed():
    rows = DEC_SEQ // GRID_W
    quarter = D_MODEL // 4
    freq = jnp.exp(-jnp.log(10000.0) * jnp.arange(quarter, dtype=F32) / quarter)[None, :]
    r = jnp.arange(rows, dtype=F32)[:, None] * freq
    col = jnp.arange(GRID_W, dtype=F32)[:, None] * freq
    rep = lambda a: jnp.repeat(a, GRID_W, axis=0)
    til = lambda a: jnp.tile(a, (rows, 1))
    return jnp.concatenate([rep(jnp.sin(r)), rep(jnp.cos(r)), til(jnp.sin(col)), til(jnp.cos(col))], -1)


def kernel(x_prompt, x_sample, c, state_hgrn, c_ctx, w_ada, b_ada, norm_mix, norm_ffn, w_in,
           lb_logits, hgrn_norm, sgu_norm, w_sgu, b_sgu, w_conv_c, w_pool, pool_scale, w_out,
           w_up, w_conv_ffn, w_down, norm_final):
    p = jax.nn.softmax(lb_logits.astype(F32), axis=0)
    lbs = jnp.cumsum(p, axis=0) - p[0:1]

    cvec = jnp.concatenate([c_ctx[None, :], c, jnp.zeros((MOD_ROWS - 1 - DEC_BATCH, D_MODEL), F32)], 0)
    mod = _adaln(cvec, w_ada, b_ada)

    consts = _hgrn_constants()
    w_sgu, w_pool = w_sgu.astype(BF16), w_pool.astype(BF16)
    states = []
    x = None
    for l in range(DEPTH):
        row = lambda a: a[l].reshape(1, -1)
        if l == 0:
            x, h = _prep(x_prompt.reshape(N_CTX, D_MODEL), x_sample.reshape(N_LAT, D_MODEL),
                         _grid_pos_embed(), row(norm_mix), mod, l)
        else:
            h = _norm_mod(x, row(norm_mix), mod, l)
        z = _inproj(h, w_in, l)
        a_ctx, st, w_up_l, w_down_l, w_out_l = _hgrn(
            z, lbs[l], row(hgrn_norm), consts, seq_len=SEQ, n_seq=BATCH, row_block0=0, layer=l,
            casts=(w_up, w_down, w_out))
        a_lat, = _hgrn(z, lbs[l], row(hgrn_norm), consts, seq_len=DEC_SEQ, n_seq=DEC_BATCH,
                       row_block0=N_CTX // DEC_SEQ, s0=state_hgrn, layer=l)
        b_sgu_full = jnp.repeat(b_sgu[l].T, HEAD_DIM, axis=1)
        x = _mix(a_ctx, a_lat, z, x, mod, row(sgu_norm), w_sgu, b_sgu_full, w_conv_c[l], w_pool,
                 row(pool_scale), w_out_l, l)
        x = _ffn(x, row(norm_ffn), mod, w_up_l, w_conv_ffn, w_down_l, l)
        states.append(st)

    gain = norm_final.reshape(1, -1)
    y_prompt = _final_norm(x, gain, 0, N_TM_CTX).reshape(BATCH, SEQ, D_MODEL)
    y_sample = _final_norm(x, gain, N_TM_CTX, N_TM - N_TM_CTX).reshape(DEC_BATCH, DEC_SEQ, D_MODEL)
    return (y_prompt, y_sample, jnp.stack(states, axis=1))
```

```python
import functools

import numpy as np
import jax
import jax.numpy as jnp
from jax import lax
from jax.experimental import pallas as pl
from jax.experimental.pallas import tpu as pltpu

F32 = jnp.float32
BF16 = jnp.bfloat16

D_MODEL = 2048
BATCH, SEQ = 16, 256
DEC_BATCH, DEC_SEQ = 2, 1024
DEPTH = 2
GRID_W = 64
HEAD_DIM = 128
W_GROUP = D_MODEL // 4
N_HEADS = W_GROUP // HEAD_DIM
POOL_WINDOWS = (2, 4, 8, 16)
POOL_HALO = 8
CHUNK_B = 128
D_FF = 5632
D_IN = 11 * W_GROUP
EPS = 1e-6

N_CTX = BATCH * SEQ
N_LAT = DEC_BATCH * DEC_SEQ
N_TOK = N_CTX + N_LAT
MOD_ROWS = 8

TM = 1024
N_TM = N_TOK // TM
N_TM_CTX = N_CTX // TM
TM_IN = 2048
TN_IN = 512
TF = 512
FFN_CT = 256
FFN_RC = 64
FFN_NOUT = 512
TMIX = 256
N_TMIX_CTX = N_CTX // TMIX
TMIX_PER_LAT = DEC_SEQ // TMIX
TP = 512
N_TP_CTX = N_CTX // TP
TP_PER_LAT = DEC_SEQ // TP
TN_ADA = 1024
NORM_ROWS = 128
CAST_ROWS = 32

HC = 64
HC_LEVELS = 6
HC_FINE_LEVELS = 3
VMEM_LIMIT = 56 * 1024 * 1024


def _silu(x):
    return x * jax.nn.sigmoid(x)


def _dot(a, b):
    return jnp.dot(a, b, preferred_element_type=F32)


def _dot_nt(a, b):
    return lax.dot_general(a, b, (((1,), (1,)), ((), ())), preferred_element_type=F32)


def _params(*sem):
    return pltpu.CompilerParams(dimension_semantics=sem, vmem_limit_bytes=VMEM_LIMIT)


def _adaln_kernel(c_ref, w_ref, b_ref, o_ref):
    s = _silu(c_ref[...]).astype(BF16)
    o_ref[...] = _dot(s, w_ref[...].astype(BF16)) + b_ref[...]


def _adaln(cvec, w_ada, b_ada):
    n_out = w_ada.shape[-1]
    return pl.pallas_call(
        _adaln_kernel,
        grid=(DEPTH, n_out // TN_ADA),
        in_specs=[
            pl.BlockSpec((MOD_ROWS, D_MODEL), lambda l, j: (0, 0)),
            pl.BlockSpec((None, D_MODEL, TN_ADA), lambda l, j: (l, 0, j)),
            pl.BlockSpec((None, 1, TN_ADA), lambda l, j: (l, 0, j)),
        ],
        out_specs=pl.BlockSpec((None, MOD_ROWS, TN_ADA), lambda l, j: (l, 0, j)),
        out_shape=jax.ShapeDtypeStruct((DEPTH, MOD_ROWS, n_out), F32),
        compiler_params=_params("arbitrary", "arbitrary"),
        name="adaln",
    )(cvec, w_ada, b_ada.reshape(DEPTH, 1, n_out))


def _norm_mod_rows(x_ref, h_ref, gain, shift, scale, n_rows):
    gain_eff = gain * (1.0 + scale)

    def body(n, carry):
        r0 = pl.multiple_of(n * NORM_ROWS, NORM_ROWS)
        x = x_ref[pl.ds(r0, NORM_ROWS), :]
        ms = jnp.mean(x * x, axis=-1, keepdims=True)
        h_ref[pl.ds(r0, NORM_ROWS), :] = (x * lax.rsqrt(ms + EPS) * gain_eff + shift).astype(BF16)
        return carry
    lax.fori_loop(0, n_rows // NORM_ROWS, body, 0)


def _mod_row_of_big_tile(i):
    return jnp.maximum(i - (N_TM_CTX - 1), 0)


def _mod_spec(layer, k):
    return pl.BlockSpec((None, MOD_ROWS, D_MODEL), lambda *_: (layer, 0, k))


def _prep_kernel(xp_ref, xs_ref, pos_ref, g_ref, sh_ref, sc_ref, x_ref, h_ref):
    i = pl.program_id(0)
    lat_tile = jnp.maximum(i - N_TP_CTX, 0)
    r = jnp.where(i < N_TP_CTX, 0, 1 + lat_tile // TP_PER_LAT)

    @pl.when(i < N_TP_CTX)
    def _():
        def body(n, carry):
            rows = pl.ds(pl.multiple_of(n * NORM_ROWS, NORM_ROWS), NORM_ROWS)
            x_ref[rows, :] = xp_ref[rows, :]
            return carry
        lax.fori_loop(0, TP // NORM_ROWS, body, 0)

    @pl.when(i >= N_TP_CTX)
    def _():
        def body(n, carry):
            rows = pl.ds(pl.multiple_of(n * NORM_ROWS, NORM_ROWS), NORM_ROWS)
            x_ref[rows, :] = xs_ref[rows, :] + pos_ref[rows, :]
            return carry
        lax.fori_loop(0, TP // NORM_ROWS, body, 0)

    _norm_mod_rows(x_ref, h_ref, g_ref[...], sh_ref[pl.ds(r, 1), :], sc_ref[pl.ds(r, 1), :], TP)


def _prep(x_prompt, x_sample, pos, gain, mod, layer):
    lat = lambda i: jnp.maximum(i - N_TP_CTX, 0)
    return pl.pallas_call(
        _prep_kernel,
        grid=(N_TOK // TP,),
        in_specs=[
            pl.BlockSpec((TP, D_MODEL), lambda i: (jnp.minimum(i, N_TP_CTX - 1), 0)),
            pl.BlockSpec((TP, D_MODEL), lambda i: (lat(i), 0)),
            pl.BlockSpec((TP, D_MODEL), lambda i: (lat(i) % TP_PER_LAT, 0)),
            pl.BlockSpec((1, D_MODEL), lambda i: (0, 0)),
            _mod_spec(layer, 0),
            _mod_spec(layer, 1),
        ],
        out_specs=[pl.BlockSpec((TP, D_MODEL), lambda i: (i, 0)),
                   pl.BlockSpec((TP, D_MODEL), lambda i: (i, 0))],
        out_shape=[jax.ShapeDtypeStruct((N_TOK, D_MODEL), F32),
                   jax.ShapeDtypeStruct((N_TOK, D_MODEL), BF16)],
        compiler_params=_params("arbitrary"),
        name="prep",
    )(x_prompt, x_sample, pos, gain, mod, mod)


def _norm_mod_kernel(x_ref, g_ref, sh_ref, sc_ref, h_ref):
    r = _mod_row_of_big_tile(pl.program_id(0))
    _norm_mod_rows(x_ref, h_ref, g_ref[...], sh_ref[pl.ds(r, 1), :], sc_ref[pl.ds(r, 1), :], TM)


def _norm_mod(x, gain, mod, layer):
    return pl.pallas_call(
        _norm_mod_kernel,
        grid=(N_TM,),
        in_specs=[
            pl.BlockSpec((TM, D_MODEL), lambda i: (i, 0)),
            pl.BlockSpec((1, D_MODEL), lambda i: (0, 0)),
            _mod_spec(layer, 0),
            _mod_spec(layer, 1),
        ],
        out_specs=pl.BlockSpec((TM, D_MODEL), lambda i: (i, 0)),
        out_shape=jax.ShapeDtypeStruct((N_TOK, D_MODEL), BF16),
        compiler_params=_params("arbitrary"),
        name="norm_mod",
    )(x, gain, mod, mod)


def _inproj_kernel(h_ref, w_ref, z_ref):
    z_ref[...] = _dot(h_ref[...], w_ref[...].astype(BF16))


def _inproj(h, w_in, layer):
    return pl.pallas_call(
        _inproj_kernel,
        grid=(N_TOK // TM_IN, D_IN // TN_IN),
        in_specs=[
            pl.BlockSpec((TM_IN, D_MODEL), lambda i, j: (i, 0)),
            pl.BlockSpec((None, D_MODEL, TN_IN), lambda i, j: (layer, 0, j)),
        ],
        out_specs=pl.BlockSpec((TM_IN, TN_IN), lambda i, j: (i, j)),
        out_shape=jax.ShapeDtypeStruct((N_TOK, D_IN), F32),
        compiler_params=_params("arbitrary", "arbitrary"),
        name="inproj",
    )(h, w_in)


def _hgrn_constants():
    t = np.arange(HC)
    run = [(t[None, :] <= t[:, None]), (t[None, :] >= t[:, None])]
    g = np.zeros((2, HC_FINE_LEVELS + 1, HC, HC), np.float32)
    uq = np.ones((2, HC_LEVELS + 1, HC), np.float32)
    bm = np.zeros((HC_LEVELS + 1, HC, HC), np.float32)
    for d in range(2):
        g[d, 0] = run[d]
        for lev in range(HC_LEVELS):
            m = 1 << lev
            base = (t // (2 * m)) * (2 * m)
            if lev < HC_FINE_LEVELS:
                g[d, lev + 1] = run[d][base + m - 1 + d]
            upper = (t % (2 * m)) >= m
            uq[d, lev] = upper if d == 0 else ~upper
    uk = 1.0 - uq
    uk[:, HC_LEVELS] = 1.0
    for lev in range(HC_LEVELS):
        m = 1 << lev
        bm[lev] = (t[:, None] // (2 * m)) == (t[None, :] // (2 * m))
    bm[HC_LEVELS] = np.eye(HC)
    pair = bm[None] * uq[:, :, :, None] * uk[:, :, None, :]
    g = g.reshape(2, (HC_FINE_LEVELS + 1) * HC, HC)
    g3 = np.concatenate([g, g, g], axis=-1)
    lanes = lambda a: np.ascontiguousarray(np.broadcast_to(a[..., None], a.shape + (HEAD_DIM,)))
    return (jnp.asarray(g3, BF16), jnp.asarray(lanes(uq[:, :HC_FINE_LEVELS]), F32),
            jnp.asarray(pair, F32))


def _hgrn_chunk(d, r0, z_ref, lb_ref, g3_ref, uq_ref, pair_ref, s_scr, o_scr):
    rows = pl.ds(r0, HC)
    lb = lb_ref[d:d + 1, :]
    f = lb + (1.0 - lb) * jax.nn.sigmoid(z_ref[rows, (2 + d) * W_GROUP:(3 + d) * W_GROUP])
    kk_all = 1.0 - f
    lf = jnp.log(f)

    l1 = lf.astype(BF16)
    r1 = lf - l1.astype(F32)
    l2 = r1.astype(BF16)
    l3 = (r1 - l2.astype(F32)).astype(BF16)
    sums = _dot(g3_ref[d], jnp.concatenate([l1, l2, l3], axis=0))

    for h in range(N_HEADS):
        cs = slice(h * HEAD_DIM, (h + 1) * HEAD_DIM)
        q = z_ref[rows, cs]
        vb = z_ref[rows, W_GROUP + h * HEAD_DIM:W_GROUP + (h + 1) * HEAD_DIM].astype(BF16)
        kk = kk_all[:, cs]
        cum = sums[0:HC, cs]

        lo, hi = (kk, q) if d == 0 else (q, kk)
        att = _dot_nt(q.astype(BF16), kk.astype(BF16)) * pair_ref[d, HC_LEVELS]
        for lev in range(HC_LEVELS):
            m = 1 << lev
            if lev < HC_FINE_LEVELS:
                bound = sums[(lev + 1) * HC:(lev + 2) * HC, cs]
                rows_qk = jnp.where(uq_ref[d, lev] > 0.5, q, kk)
            else:
                bound = jnp.concatenate(
                    [jnp.broadcast_to(cum[b0 + m - 1 + d:b0 + m + d, :], (2 * m, HEAD_DIM))
                     for b0 in range(0, HC, 2 * m)], axis=0)
                rows_qk = jnp.concatenate(
                    [part for b0 in range(0, HC, 2 * m)
                     for part in (lo[b0:b0 + m], hi[b0 + m:b0 + 2 * m])], axis=0)
            decayed = (rows_qk * jnp.exp(-jnp.abs(cum - bound))).astype(BF16)
            att = att + _dot_nt(decayed, decayed) * pair_ref[d, lev]

        far = cum[HC - 1:HC, :] if d == 0 else cum[0:1, :]
        s_old = s_scr[d * N_HEADS + h]
        q_in = (q * jnp.exp(cum)).astype(BF16)
        o_scr[rows, cs] = _dot(jnp.concatenate([q_in, att.astype(BF16)], axis=1),
                               jnp.concatenate([s_old.astype(BF16), vb], axis=0))

        k_out = kk * jnp.exp(-jnp.abs(cum - far))
        k_ext = jnp.concatenate([k_out, jnp.broadcast_to(jnp.exp(far), (8, HEAD_DIM))], axis=0)
        k_ext_t = k_ext.T
        s_scr[d * N_HEADS + h] = (k_ext_t[:, HC:HC + 1] * s_old
                                  + _dot(k_ext_t[:, 0:HC].astype(BF16), vb))


def _hgrn_kernel(*refs, n_chunks, has_s0, n_cast):
    z_ref, lb_ref, ng_ref, g3_ref, uq_ref, pair_ref = refs[:6]
    k = 6
    s0_ref = st_ref = None
    if has_s0:
        s0_ref = refs[k]
        k += 1
    cast_src = refs[k:k + n_cast]
    k += n_cast
    a_ref = refs[k]
    k += 1
    if not has_s0:
        st_ref = refs[k]
        k += 1
    cast_dst = refs[k:k + n_cast]
    k += n_cast
    s_scr, of_scr, ob_scr = refs[k:k + 3]

    for src, dst in zip(cast_src, cast_dst):
        for r0 in range(0, src.shape[0], CAST_ROWS):
            dst[r0:r0 + CAST_ROWS, :] = src[r0:r0 + CAST_ROWS, :].astype(BF16)

    for d in range(2):
        for h in range(N_HEADS):
            s_scr[d * N_HEADS + h] = (s0_ref[d, h] if has_s0
                                      else jnp.zeros((HEAD_DIM, HEAD_DIM), F32))

    def scan_body(n, carry):
        for d, o_scr in ((0, of_scr), (1, ob_scr)):
            r0 = pl.multiple_of((n if d == 0 else n_chunks - 1 - n) * HC, HC)
            _hgrn_chunk(d, r0, z_ref, lb_ref, g3_ref, uq_ref, pair_ref, s_scr, o_scr)
        return carry
    lax.fori_loop(0, n_chunks, scan_body, 0)

    def out_body(n, carry):
        rows = pl.ds(pl.multiple_of(n * HC, HC), HC)
        o = of_scr[rows, :] + ob_scr[rows, :]
        parts = []
        for h in range(N_HEADS):
            oh = o[:, h * HEAD_DIM:(h + 1) * HEAD_DIM]
            parts.append(oh * lax.rsqrt(jnp.mean(oh * oh, axis=-1, keepdims=True) + EPS))
        gate = z_ref[rows, 4 * W_GROUP:5 * W_GROUP]
        a_ref[rows, :] = jnp.concatenate(parts, axis=-1) * ng_ref[...] * _silu(gate)
        return carry
    lax.fori_loop(0, n_chunks, out_body, 0)

    if st_ref is not None:
        for d in range(2):
            for h in range(N_HEADS):
                st_ref[d, h] = s_scr[d * N_HEADS + h]


def _hgrn(z, lb, norm_g, consts, *, seq_len, n_seq, row_block0, s0=None, layer=0, casts=()):
    has_s0 = s0 is not None
    const_specs = [pl.BlockSpec(c.shape, lambda b, nd=c.ndim: (0,) * nd) for c in consts]
    in_specs = [
        pl.BlockSpec((seq_len, 5 * W_GROUP), lambda b: (row_block0 + b, 0)),
        pl.BlockSpec((2, W_GROUP), lambda b: (0, 0)),
        pl.BlockSpec((1, W_GROUP), lambda b: (0, 0)),
    ] + const_specs
    args = [z, lb, norm_g, *consts]
    if has_s0:
        in_specs.append(pl.BlockSpec((None, None, 2, N_HEADS, HEAD_DIM, HEAD_DIM),
                                     lambda b: (b, layer, 0, 0, 0, 0)))
        args.append(s0)
    out_specs = [pl.BlockSpec((seq_len, W_GROUP), lambda b: (b, 0))]
    out_shape = [jax.ShapeDtypeStruct((n_seq * seq_len, W_GROUP), F32)]
    if not has_s0:
        out_specs.append(pl.BlockSpec((None, 2, N_HEADS, HEAD_DIM, HEAD_DIM),
                                      lambda b: (b, 0, 0, 0, 0)))
        out_shape.append(jax.ShapeDtypeStruct((n_seq, 2, N_HEADS, HEAD_DIM, HEAD_DIM), F32))
    for w in casts:
        slab, cols = w.shape[1] // n_seq, w.shape[2]
        assert slab * n_seq == w.shape[1] and slab % CAST_ROWS == 0
        in_specs.append(pl.BlockSpec((None, slab, cols), lambda b: (layer, b, 0)))
        args.append(w)
        out_specs.append(pl.BlockSpec((slab, cols), lambda b: (b, 0)))
        out_shape.append(jax.ShapeDtypeStruct(w.shape[1:], BF16))
    kern = functools.partial(_hgrn_kernel, n_chunks=seq_len // HC, has_s0=has_s0,
                             n_cast=len(casts))
    return pl.pallas_call(
        kern,
        grid=(n_seq,),
        in_specs=in_specs,
        out_specs=out_specs,
        out_shape=out_shape,
        scratch_shapes=[
            pltpu.VMEM((2 * N_HEADS, HEAD_DIM, HEAD_DIM), F32),
            pltpu.VMEM((seq_len, W_GROUP), F32),
            pltpu.VMEM((seq_len, W_GROUP), F32),
        ],
        compiler_params=_params("arbitrary"),
        name="hgrn_lat" if has_s0 else "hgrn_ctx",
    )(*args)


def _mix_kernel(actx_ref, alat_ref, bu_ref, bv_ref, cb_ref, cc_ref, ch_ref, dx_ref,
                ccp_ref, chp_ref, dxp_ref, ccn_ref, chn_ref, dxn_ref,
                x_ref, g1_ref, sgun_ref, wsgu_ref, bsgu_ref, wconv_ref, wpool_ref, pscale_ref,
                wout_ref, o_ref, cat_scr):
    i = pl.program_id(0)
    is_ctx = i < N_TMIX_CTX
    lat_tile = jnp.maximum(i - N_TMIX_CTX, 0)
    j = jnp.where(is_ctx, 0, lat_tile % TMIX_PER_LAT)
    tiles_in_seq = jnp.where(is_ctx, 1, TMIX_PER_LAT)
    keep_prev = jnp.where(j == 0, 0.0, 1.0)
    keep_next = jnp.where(j == tiles_in_seq - 1, 0.0, 1.0)
    mod_row = jnp.where(is_ctx, 0, 1 + lat_tile // TMIX_PER_LAT)
    row = lax.broadcasted_iota(jnp.int32, (TMIX, 1), 0)

    def project(k):
        ks = slice(k * W_GROUP, (k + 1) * W_GROUP)
        return _dot(cat_scr[:, ks], wout_ref[ks, :])

    cat_scr[:, 0:W_GROUP] = jnp.where(is_ctx, actx_ref[...], alat_ref[...]).astype(BF16)
    o_ref[...] = project(0)

    u = jax.nn.gelu(bu_ref[...])
    vv = jax.nn.gelu(bv_ref[...])
    vn = (vv * lax.rsqrt(jnp.mean(vv * vv, axis=-1, keepdims=True) + EPS) * sgun_ref[...]).astype(BF16)
    for n in range(TMIX // CHUNK_B):
        rs = slice(n * CHUNK_B, (n + 1) * CHUNK_B)
        for h in range(N_HEADS):
            cs = slice(h * HEAD_DIM, (h + 1) * HEAD_DIM)
            mixed = _dot(wsgu_ref[h], vn[rs, cs]) + bsgu_ref[:, cs]
            cat_scr[rs, W_GROUP + h * HEAD_DIM:W_GROUP + (h + 1) * HEAD_DIM] = (u[rs, cs] * mixed).astype(BF16)
    o_ref[...] += project(1)

    p = cc_ref[...] * ch_ref[...]
    p_edge_prev = ccp_ref[POOL_HALO - 1:POOL_HALO, :] * chp_ref[POOL_HALO - 1:POOL_HALO, :] * keep_prev
    p_edge_next = ccn_ref[0:1, :] * chn_ref[0:1, :] * keep_next
    p_prev = jnp.where(row == 0, p_edge_prev, pltpu.roll(p, 1, 0))
    p_next = jnp.where(row == TMIX - 1, p_edge_next, pltpu.roll(p, TMIX - 1, 0))
    wc = wconv_ref[...]
    c_out = cb_ref[...] * (wc[0:1] * p_prev + wc[1:2] * p + wc[2:3] * p_next)
    cat_scr[:, 2 * W_GROUP:3 * W_GROUP] = c_out.astype(BF16)
    o_ref[...] += project(2)

    xd = dx_ref[...]
    ext = jnp.concatenate([dxp_ref[...] * keep_prev, xd, dxn_ref[...] * keep_next], axis=0)
    n_ext = TMIX + 2 * POOL_HALO
    back = lambda a, s: pltpu.roll(a, s, 0)
    ahead = lambda a, s: pltpu.roll(a, n_ext - s, 0)
    sums = [back(ext, 1) + ext]
    for s in (1, 2, 4):
        sums.append(back(sums[-1], s) + ahead(sums[-1], s))
    pos = j * TMIX + row
    seq_len = tiles_in_seq * TMIX
    gd = W_GROUP // len(POOL_WINDOWS)
    for g, win in enumerate(POOL_WINDOWS):
        cs = slice(g * gd, (g + 1) * gd)
        cnt = (jnp.minimum(pos + win // 2, seq_len) - jnp.maximum(pos - win // 2, 0)).astype(F32)
        mean = sums[g][POOL_HALO:POOL_HALO + TMIX, cs] / cnt
        dg = _dot((mean - xd[:, cs]).astype(BF16), wpool_ref[g]) * pscale_ref[:, cs]
        cat_scr[:, 3 * W_GROUP + g * gd:3 * W_GROUP + (g + 1) * gd] = dg.astype(BF16)

    y = o_ref[...] + project(3)
    o_ref[...] = x_ref[...] + g1_ref[pl.ds(mod_row, 1), :] * y


def _mix(a_ctx, a_lat, z, x, mod, sgu_norm, w_sgu, b_sgu_full, w_conv, w_pool, pool_scale, w_out,
         layer):
    halo_blocks = TMIX // POOL_HALO
    last_halo = N_TOK // POOL_HALO - 1
    zcol = lambda k: pl.BlockSpec((TMIX, W_GROUP), lambda i: (i, k))
    zprev = lambda k: pl.BlockSpec((POOL_HALO, W_GROUP),
                                   lambda i: (jnp.maximum(i * halo_blocks - 1, 0), k))
    znext = lambda k: pl.BlockSpec((POOL_HALO, W_GROUP),
                                   lambda i: (jnp.minimum((i + 1) * halo_blocks, last_halo), k))
    full = lambda arr: pl.BlockSpec(arr.shape, lambda i, nd=arr.ndim: (0,) * nd)
    of_layer = lambda arr: pl.BlockSpec((None,) + arr.shape[1:],
                                        lambda i, nd=arr.ndim: (layer,) + (0,) * (nd - 1))
    return pl.pallas_call(
        _mix_kernel,
        grid=(N_TOK // TMIX,),
        in_specs=[
            pl.BlockSpec((TMIX, W_GROUP), lambda i: (jnp.minimum(i, N_TMIX_CTX - 1), 0)),
            pl.BlockSpec((TMIX, W_GROUP), lambda i: (jnp.maximum(i - N_TMIX_CTX, 0), 0)),
            zcol(5), zcol(6), zcol(7), zcol(8), zcol(9), zcol(10),
            zprev(8), zprev(9), zprev(10), znext(8), znext(9), znext(10),
            pl.BlockSpec((TMIX, D_MODEL), lambda i: (i, 0)),
            _mod_spec(layer, 2),
            full(sgu_norm), of_layer(w_sgu), full(b_sgu_full), full(w_conv), of_layer(w_pool),
            full(pool_scale), full(w_out),
        ],
        out_specs=pl.BlockSpec((TMIX, D_MODEL), lambda i: (i, 0)),
        out_shape=jax.ShapeDtypeStruct((N_TOK, D_MODEL), F32),
        scratch_shapes=[pltpu.VMEM((TMIX, D_MODEL), BF16)],
        compiler_params=_params("arbitrary"),
        name="mix",
    )(a_ctx, a_lat, z, z, z, z, z, z, z, z, z, z, z, z, x, mod, sgu_norm, w_sgu, b_sgu_full, w_conv, w_pool,
      pool_scale, w_out)


def _ffn_kernel(x_ref, g_ref, sh_ref, sc_ref, g2_ref, wv_ref, wg_ref, cv_ref, cg_ref, wd_ref,
                o_ref, h_scr, uv_scr, ug_scr, act_scr):
    i = pl.program_id(0)
    j = pl.program_id(1)
    r = _mod_row_of_big_tile(i)

    @pl.when(j == 0)
    def _():
        _norm_mod_rows(x_ref, h_scr, g_ref[...], sh_ref[pl.ds(r, 1), :], sc_ref[pl.ds(r, 1), :], TM)

        def zero_body(n, carry):
            rows = pl.ds(pl.multiple_of(n * NORM_ROWS, NORM_ROWS), NORM_ROWS)
            o_ref[rows, :] = jnp.zeros((NORM_ROWS, D_MODEL), F32)
            return carry
        lax.fori_loop(0, TM // NORM_ROWS, zero_body, 0)

    inner_keep = jnp.where(i < N_TM_CTX, 0.0, 1.0)

    def conv3_rows(u_scr, w, r0, cs):
        cur = u_scr[r0:r0 + FFN_RC, cs]
        if r0 == 0:
            prev = jnp.zeros((POOL_HALO, FFN_CT), F32)
        else:
            prev = u_scr[r0 - POOL_HALO:r0, cs]
            if r0 % SEQ == 0:
                prev = prev * inner_keep
        r1 = r0 + FFN_RC
        if r1 == TM:
            nxt = jnp.zeros((POOL_HALO, FFN_CT), F32)
        else:
            nxt = u_scr[r1:r1 + POOL_HALO, cs]
            if r1 % SEQ == 0:
                nxt = nxt * inner_keep
        ext = jnp.concatenate([prev, cur, nxt], axis=0)
        n_ext = FFN_RC + 2 * POOL_HALO
        before = pltpu.roll(ext, 1, 0)[POOL_HALO:POOL_HALO + FFN_RC]
        after = pltpu.roll(ext, n_ext - 1, 0)[POOL_HALO:POOL_HALO + FFN_RC]
        return w[0:1] * before + w[1:2] * cur + w[2:3] * after

    col_tiles = [slice(c * FFN_CT, (c + 1) * FFN_CT) for c in range(TF // FFN_CT)]
    for cs in col_tiles:
        uv_scr[:, cs] = _dot(h_scr[...], wv_ref[:, cs])
        ug_scr[:, cs] = _dot(h_scr[...], wg_ref[:, cs])
    for cs in col_tiles:
        cv = cv_ref[:, cs]
        cg = cg_ref[:, cs]
        for r0 in range(0, TM, FFN_RC):
            val = conv3_rows(uv_scr, cv, r0, cs)
            gate = conv3_rows(ug_scr, cg, r0, cs)
            act_scr[r0:r0 + FFN_RC, cs] = (val * _silu(gate)).astype(BF16)
    for cs in col_tiles:
        for c in range(D_MODEL // FFN_NOUT):
            ns = slice(c * FFN_NOUT, (c + 1) * FFN_NOUT)
            o_ref[:, ns] += _dot(act_scr[:, cs], wd_ref[cs, ns])

    @pl.when(j == pl.num_programs(1) - 1)
    def _():
        g2 = g2_ref[pl.ds(r, 1), :]

        def body(n, carry):
            rows = pl.ds(pl.multiple_of(n * NORM_ROWS, NORM_ROWS), NORM_ROWS)
            o_ref[rows, :] = x_ref[rows, :] + g2 * o_ref[rows, :]
            return carry
        lax.fori_loop(0, TM // NORM_ROWS, body, 0)


def _ffn(x, gain, mod, w_up, w_conv, w_down, layer):
    n_f = D_FF // TF
    return pl.pallas_call(
        _ffn_kernel,
        grid=(N_TM, n_f),
        in_specs=[
            pl.BlockSpec((TM, D_MODEL), lambda i, j: (i, 0), pipeline_mode=pl.Buffered(1)),
            pl.BlockSpec((1, D_MODEL), lambda i, j: (0, 0)),
            _mod_spec(layer, 3),
            _mod_spec(layer, 4),
            _mod_spec(layer, 5),
            pl.BlockSpec((D_MODEL, TF), lambda i, j: (0, j)),
            pl.BlockSpec((D_MODEL, TF), lambda i, j: (0, j + n_f)),
            pl.BlockSpec((None, 3, TF), lambda i, j: (layer, 0, j)),
            pl.BlockSpec((None, 3, TF), lambda i, j: (layer, 0, j + n_f)),
            pl.BlockSpec((TF, D_MODEL), lambda i, j: (j, 0)),
        ],
        out_specs=pl.BlockSpec((TM, D_MODEL), lambda i, j: (i, 0)),
        out_shape=jax.ShapeDtypeStruct((N_TOK, D_MODEL), F32),
        scratch_shapes=[pltpu.VMEM((TM, D_MODEL), BF16), pltpu.VMEM((TM, TF), F32),
                        pltpu.VMEM((TM, TF), F32), pltpu.VMEM((TM, TF), BF16)],
        compiler_params=_params("arbitrary", "arbitrary"),
        name="ffn",
    )(x, gain, mod, mod, mod, w_up, w_up, w_conv, w_conv, w_down)


def _final_norm_kernel(x_ref, g_ref, o_ref):
    gain = g_ref[...]

    def body(n, carry):
        rows = pl.ds(pl.multiple_of(n * NORM_ROWS, NORM_ROWS), NORM_ROWS)
        x = x_ref[rows, :]
        o_ref[rows, :] = x * lax.rsqrt(jnp.mean(x * x, axis=-1, keepdims=True) + EPS) * gain
        return carry
    lax.fori_loop(0, TM // NORM_ROWS, body, 0)


def _final_norm(x, gain, tile0, n_tiles):
    return pl.pallas_call(
        _final_norm_kernel,
        grid=(n_tiles,),
        in_specs=[pl.BlockSpec((TM, D_MODEL), lambda i: (tile0 + i, 0)),
                  pl.BlockSpec((1, D_MODEL), lambda i: (0, 0))],
        out_specs=pl.BlockSpec((TM, D_MODEL), lambda i: (i, 0)),
        out_shape=jax.ShapeDtypeStruct((n_tiles * TM, D_MODEL), F32),
        compiler_params=_params("arbitrary"),
        name="final_norm",
    )(x, gain)


def _grid_pos_embed():
    rows = DEC_SEQ // GRID_W
    quarter = D_MODEL // 4
    freq = jnp.exp(-jnp.log(10000.0) * jnp.arange(quarter, dtype=F32) / quarter)[None, :]
    r = jnp.arange(rows, dtype=F32)[:, None] * freq
    col = jnp.arange(GRID_W, dtype=F32)[:, None] * freq
    rep = lambda a: jnp.repeat(a, GRID_W, axis=0)
    til = lambda a: jnp.tile(a, (rows, 1))
    return jnp.concatenate([rep(jnp.sin(r)), rep(jnp.cos(r)), til(jnp.sin(col)), til(jnp.cos(col))], -1)


def kernel(x_prompt, x_sample, c, state_hgrn, c_ctx, w_ada, b_ada, norm_mix, norm_ffn, w_in,
           lb_logits, hgrn_norm, sgu_norm, w_sgu, b_sgu, w_conv_c, w_pool, pool_scale, w_out,
           w_up, w_conv_ffn, w_down, norm_final):
    p = jax.nn.softmax(lb_logits.astype(F32), axis=0)
    lbs = jnp.cumsum(p, axis=0) - p[0:1]

    cvec = jnp.concatenate([c_ctx[None, :], c, jnp.zeros((MOD_ROWS - 1 - DEC_BATCH, D_MODEL), F32)], 0)
    mod = _adaln(cvec, w_ada, b_ada)

    consts = _hgrn_constants()
    w_sgu, w_pool = w_sgu.astype(BF16), w_pool.astype(BF16)
    states = []
    x = None
    for l in range(DEPTH):
        row = lambda a: a[l].reshape(1, -1)
        if l == 0:
            x, h = _prep(x_prompt.reshape(N_CTX, D_MODEL), x_sample.reshape(N_LAT, D_MODEL),
                         _grid_pos_embed(), row(norm_mix), mod, l)
        else:
            h = _norm_mod(x, row(norm_mix), mod, l)
        z = _inproj(h, w_in, l)
        a_ctx, st, w_up_l, w_down_l, w_out_l = _hgrn(
            z, lbs[l], row(hgrn_norm), consts, seq_len=SEQ, n_seq=BATCH, row_block0=0, layer=l,
            casts=(w_up, w_down, w_out))
        a_lat, = _hgrn(z, lbs[l], row(hgrn_norm), consts, seq_len=DEC_SEQ, n_seq=DEC_BATCH,
                       row_block0=N_CTX // DEC_SEQ, s0=state_hgrn, layer=l)
        b_sgu_full = jnp.repeat(b_sgu[l].T, HEAD_DIM, axis=1)
        x = _mix(a_ctx, a_lat, z, x, mod, row(sgu_norm), w_sgu, b_sgu_full, w_conv_c[l], w_pool,
                 row(pool_scale), w_out_l, l)
        x = _ffn(x, row(norm_ffn), mod, w_up_l, w_conv_ffn, w_down_l, l)
        states.append(st)

    gain = norm_final.reshape(1, -1)
    y_prompt = _final_norm(x, gain, 0, N_TM_CTX).reshape(BATCH, SEQ, D_MODEL)
    y_sample = _final_norm(x, gain, N_TM_CTX, N_TM - N_TM_CTX).reshape(DEC_BATCH, DEC_SEQ, D_MODEL)
    return (y_prompt, y_sample, jnp.stack(states, axis=1))
```

```python
import functools

import numpy as np
import jax
import jax.numpy as jnp
from jax import lax
from jax.experimental import pallas as pl
from jax.experimental.pallas import tpu as pltpu

F32 = jnp.float32
BF16 = jnp.bfloat16

D_MODEL = 2048
BATCH, SEQ = 16, 256
DEC_BATCH, DEC_SEQ = 2, 1024
DEPTH = 2
GRID_W = 64
HEAD_DIM = 128
W_GROUP = D_MODEL // 4
N_HEADS = W_GROUP // HEAD_DIM
POOL_WINDOWS = (2, 4, 8, 16)
POOL_HALO = 8
CHUNK_B = 128
D_FF = 5632
D_IN = 11 * W_GROUP
EPS = 1e-6

N_CTX = BATCH * SEQ
N_LAT = DEC_BATCH * DEC_SEQ
N_TOK = N_CTX + N_LAT
MOD_ROWS = 8

TM = 1024
N_TM = N_TOK // TM
N_TM_CTX = N_CTX // TM
TM_IN = 2048
TN_IN = 512
TF = 512
FFN_CT = 256
FFN_RC = 64
FFN_NOUT = 512
TMIX = 256
N_TMIX_CTX = N_CTX // TMIX
TMIX_PER_LAT = DEC_SEQ // TMIX
TP = 512
N_TP_CTX = N_CTX // TP
TP_PER_LAT = DEC_SEQ // TP
TN_ADA = 1024
NORM_ROWS = 128
CAST_ROWS = 32

HC = 64
HC_LEVELS = 6
HC_FINE_LEVELS = 3
VMEM_LIMIT = 56 * 1024 * 1024


def _silu(x):
    return x * jax.nn.sigmoid(x)


def _dot(a, b):
    return jnp.dot(a, b, preferred_element_type=F32)


def _dot_nt(a, b):
    return lax.dot_general(a, b, (((1,), (1,)), ((), ())), preferred_element_type=F32)


def _params(*sem):
    return pltpu.CompilerParams(dimension_semantics=sem, vmem_limit_bytes=VMEM_LIMIT)


def _adaln_kernel(c_ref, w_ref, b_ref, o_ref):
    s = _silu(c_ref[...]).astype(BF16)
    o_ref[...] = _dot(s, w_ref[...].astype(BF16)) + b_ref[...]


def _adaln(cvec, w_ada, b_ada):
    n_out = w_ada.shape[-1]
    return pl.pallas_call(
        _adaln_kernel,
        grid=(DEPTH, n_out // TN_ADA),
        in_specs=[
            pl.BlockSpec((MOD_ROWS, D_MODEL), lambda l, j: (0, 0)),
            pl.BlockSpec((None, D_MODEL, TN_ADA), lambda l, j: (l, 0, j)),
            pl.BlockSpec((None, 1, TN_ADA), lambda l, j: (l, 0, j)),
        ],
        out_specs=pl.BlockSpec((None, MOD_ROWS, TN_ADA), lambda l, j: (l, 0, j)),
        out_shape=jax.ShapeDtypeStruct((DEPTH, MOD_ROWS, n_out), F32),
        compiler_params=_params("arbitrary", "arbitrary"),
        name="adaln",
    )(cvec, w_ada, b_ada.reshape(DEPTH, 1, n_out))


def _norm_mod_rows(x_ref, h_ref, gain, shift, scale, n_rows):
    gain_eff = gain * (1.0 + scale)

    def body(n, carry):
        r0 = pl.multiple_of(n * NORM_ROWS, NORM_ROWS)
        x = x_ref[pl.ds(r0, NORM_ROWS), :]
        ms = jnp.mean(x * x, axis=-1, keepdims=True)
        h_ref[pl.ds(r0, NORM_ROWS), :] = (x * lax.rsqrt(ms + EPS) * gain_eff + shift).astype(BF16)
        return carry
    lax.fori_loop(0, n_rows // NORM_ROWS, body, 0)


def _mod_row_of_big_tile(i):
    return jnp.maximum(i - (N_TM_CTX - 1), 0)


def _mod_spec(layer, k):
    return pl.BlockSpec((None, MOD_ROWS, D_MODEL), lambda *_: (layer, 0, k))


def _prep_kernel(xp_ref, xs_ref, pos_ref, g_ref, sh_ref, sc_ref, x_ref, h_ref):
    i = pl.program_id(0)
    lat_tile = jnp.maximum(i - N_TP_CTX, 0)
    r = jnp.where(i < N_TP_CTX, 0, 1 + lat_tile // TP_PER_LAT)

    @pl.when(i < N_TP_CTX)
    def _():
        def body(n, carry):
            rows = pl.ds(pl.multiple_of(n * NORM_ROWS, NORM_ROWS), NORM_ROWS)
            x_ref[rows, :] = xp_ref[rows, :]
            return carry
        lax.fori_loop(0, TP // NORM_ROWS, body, 0)

    @pl.when(i >= N_TP_CTX)
    def _():
        def body(n, carry):
            rows = pl.ds(pl.multiple_of(n * NORM_ROWS, NORM_ROWS), NORM_ROWS)
            x_ref[rows, :] = xs_ref[rows, :] + pos_ref[rows, :]
            return carry
        lax.fori_loop(0, TP // NORM_ROWS, body, 0)

    _norm_mod_rows(x_ref, h_ref, g_ref[...], sh_ref[pl.ds(r, 1), :], sc_ref[pl.ds(r, 1), :], TP)


def _prep(x_prompt, x_sample, pos, gain, mod, layer):
    lat = lambda i: jnp.maximum(i - N_TP_CTX, 0)
    return pl.pallas_call(
        _prep_kernel,
        grid=(N_TOK // TP,),
        in_specs=[
            pl.BlockSpec((TP, D_MODEL), lambda i: (jnp.minimum(i, N_TP_CTX - 1), 0)),
            pl.BlockSpec((TP, D_MODEL), lambda i: (lat(i), 0)),
            pl.BlockSpec((TP, D_MODEL), lambda i: (lat(i) % TP_PER_LAT, 0)),
            pl.BlockSpec((1, D_MODEL), lambda i: (0, 0)),
            _mod_spec(layer, 0),
            _mod_spec(layer, 1),
        ],
        out_specs=[pl.BlockSpec((TP, D_MODEL), lambda i: (i, 0)),
                   pl.BlockSpec((TP, D_MODEL), lambda i: (i, 0))],
        out_shape=[jax.ShapeDtypeStruct((N_TOK, D_MODEL), F32),
                   jax.ShapeDtypeStruct((N_TOK, D_MODEL), BF16)],
        compiler_params=_params("arbitrary"),
        name="prep",
    )(x_prompt, x_sample, pos, gain, mod, mod)


def _inproj_kernel(h_ref, w_ref, z_ref):
    z_ref[...] = _dot(h_ref[...], w_ref[...].astype(BF16))


def _inproj(h, w_in, layer):
    return pl.pallas_call(
        _inproj_kernel,
        grid=(N_TOK // TM_IN, D_IN // TN_IN),
        in_specs=[
            pl.BlockSpec((TM_IN, D_MODEL), lambda i, j: (i, 0)),
            pl.BlockSpec((None, D_MODEL, TN_IN), lambda i, j: (layer, 0, j)),
        ],
        out_specs=pl.BlockSpec((TM_IN, TN_IN), lambda i, j: (i, j)),
        out_shape=jax.ShapeDtypeStruct((N_TOK, D_IN), F32),
        compiler_params=_params("arbitrary", "arbitrary"),
        name="inproj",
    )(h, w_in)


def _hgrn_constants():
    t = np.arange(HC)
    run = [(t[None, :] <= t[:, None]), (t[None, :] >= t[:, None])]
    g = np.zeros((2, HC_FINE_LEVELS + 1, HC, HC), np.float32)
    uq = np.ones((2, HC_LEVELS + 1, HC), np.float32)
    bm = np.zeros((HC_LEVELS + 1, HC, HC), np.float32)
    for d in range(2):
        g[d, 0] = run[d]
        for lev in range(HC_LEVELS):
            m = 1 << lev
            base = (t // (2 * m)) * (2 * m)
            if lev < HC_FINE_LEVELS:
                g[d, lev + 1] = run[d][base + m - 1 + d]
            upper = (t % (2 * m)) >= m
            uq[d, lev] = upper if d == 0 else ~upper
    uk = 1.0 - uq
    uk[:, HC_LEVELS] = 1.0
    for lev in range(HC_LEVELS):
        m = 1 << lev
        bm[lev] = (t[:, None] // (2 * m)) == (t[None, :] // (2 * m))
    bm[HC_LEVELS] = np.eye(HC)
    pair = bm[None] * uq[:, :, :, None] * uk[:, :, None, :]
    g = g.reshape(2, (HC_FINE_LEVELS + 1) * HC, HC)
    g3 = np.concatenate([g, g, g], axis=-1)
    lanes = lambda a: np.ascontiguousarray(np.broadcast_to(a[..., None], a.shape + (HEAD_DIM,)))
    return (jnp.asarray(g3, BF16), jnp.asarray(lanes(uq[:, :HC_FINE_LEVELS]), F32),
            jnp.asarray(pair, F32))


def _hgrn_chunk(d, r0, z_ref, lb_ref, g3_ref, uq_ref, pair_ref, s_scr, o_scr):
    rows = pl.ds(r0, HC)
    lb = lb_ref[d:d + 1, :]
    f = lb + (1.0 - lb) * jax.nn.sigmoid(z_ref[rows, (2 + d) * W_GROUP:(3 + d) * W_GROUP])
    kk_all = 1.0 - f
    lf = jnp.log(f)

    l1 = lf.astype(BF16)
    r1 = lf - l1.astype(F32)
    l2 = r1.astype(BF16)
    l3 = (r1 - l2.astype(F32)).astype(BF16)
    sums = _dot(g3_ref[d], jnp.concatenate([l1, l2, l3], axis=0))

    for h in range(N_HEADS):
        cs = slice(h * HEAD_DIM, (h + 1) * HEAD_DIM)
        q = z_ref[rows, cs]
        vb = z_ref[rows, W_GROUP + h * HEAD_DIM:W_GROUP + (h + 1) * HEAD_DIM].astype(BF16)
        kk = kk_all[:, cs]
        cum = sums[0:HC, cs]

        lo, hi = (kk, q) if d == 0 else (q, kk)
        att = _dot_nt(q.astype(BF16), kk.astype(BF16)) * pair_ref[d, HC_LEVELS]
        for lev in range(HC_LEVELS):
            m = 1 << lev
            if lev < HC_FINE_LEVELS:
                bound = sums[(lev + 1) * HC:(lev + 2) * HC, cs]
                rows_qk = jnp.where(uq_ref[d, lev] > 0.5, q, kk)
            else:
                bound = jnp.concatenate(
                    [jnp.broadcast_to(cum[b0 + m - 1 + d:b0 + m + d, :], (2 * m, HEAD_DIM))
                     for b0 in range(0, HC, 2 * m)], axis=0)
                rows_qk = jnp.concatenate(
                    [part for b0 in range(0, HC, 2 * m)
                     for part in (lo[b0:b0 + m], hi[b0 + m:b0 + 2 * m])], axis=0)
            decayed = (rows_qk * jnp.exp(-jnp.abs(cum - bound))).astype(BF16)
            att = att + _dot_nt(decayed, decayed) * pair_ref[d, lev]

        far = cum[HC - 1:HC, :] if d == 0 else cum[0:1, :]
        s_old = s_scr[d * N_HEADS + h]
        q_in = (q * jnp.exp(cum)).astype(BF16)
        o_scr[rows, cs] = _dot(jnp.concatenate([q_in, att.astype(BF16)], axis=1),
                               jnp.concatenate([s_old.astype(BF16), vb], axis=0))

        k_out = kk * jnp.exp(-jnp.abs(cum - far))
        k_ext = jnp.concatenate([k_out, jnp.broadcast_to(jnp.exp(far), (8, HEAD_DIM))], axis=0)
        k_ext_t = k_ext.T
        s_scr[d * N_HEADS + h] = (k_ext_t[:, HC:HC + 1] * s_old
                                  + _dot(k_ext_t[:, 0:HC].astype(BF16), vb))


def _hgrn_kernel(*refs, n_chunks, has_s0, n_cast):
    z_ref, lb_ref, ng_ref, g3_ref, uq_ref, pair_ref = refs[:6]
    k = 6
    s0_ref = st_ref = None
    if has_s0:
        s0_ref = refs[k]
        k += 1
    cast_src = refs[k:k + n_cast]
    k += n_cast
    a_ref = refs[k]
    k += 1
    if not has_s0:
        st_ref = refs[k]
        k += 1
    cast_dst = refs[k:k + n_cast]
    k += n_cast
    s_scr, of_scr, ob_scr = refs[k:k + 3]

    for src, dst in zip(cast_src, cast_dst):
        for r0 in range(0, src.shape[0], CAST_ROWS):
            dst[r0:r0 + CAST_ROWS, :] = src[r0:r0 + CAST_ROWS, :].astype(BF16)

    for d in range(2):
        for h in range(N_HEADS):
            s_scr[d * N_HEADS + h] = (s0_ref[d, h] if has_s0
                                      else jnp.zeros((HEAD_DIM, HEAD_DIM), F32))

    def scan_body(n, carry):
        for d, o_scr in ((0, of_scr), (1, ob_scr)):
            r0 = pl.multiple_of((n if d == 0 else n_chunks - 1 - n) * HC, HC)
            _hgrn_chunk(d, r0, z_ref, lb_ref, g3_ref, uq_ref, pair_ref, s_scr, o_scr)
        return carry
    lax.fori_loop(0, n_chunks, scan_body, 0)

    def out_body(n, carry):
        rows = pl.ds(pl.multiple_of(n * HC, HC), HC)
        o = of_scr[rows, :] + ob_scr[rows, :]
        parts = []
        for h in range(N_HEADS):
            oh = o[:, h * HEAD_DIM:(h + 1) * HEAD_DIM]
            parts.append(oh * lax.rsqrt(jnp.mean(oh * oh, axis=-1, keepdims=True) + EPS))
        gate = z_ref[rows, 4 * W_GROUP:5 * W_GROUP]
        a_ref[rows, :] = jnp.concatenate(parts, axis=-1) * ng_ref[...] * _silu(gate)
        return carry
    lax.fori_loop(0, n_chunks, out_body, 0)

    if st_ref is not None:
        for d in range(2):
            for h in range(N_HEADS):
                st_ref[d, h] = s_scr[d * N_HEADS + h]


def _hgrn(z, lb, norm_g, consts, *, seq_len, n_seq, row_block0, s0=None, layer=0, casts=()):
    has_s0 = s0 is not None
    const_specs = [pl.BlockSpec(c.shape, lambda b, nd=c.ndim: (0,) * nd) for c in consts]
    in_specs = [
        pl.BlockSpec((seq_len, 5 * W_GROUP), lambda b: (row_block0 + b, 0)),
        pl.BlockSpec((2, W_GROUP), lambda b: (0, 0)),
        pl.BlockSpec((1, W_GROUP), lambda b: (0, 0)),
    ] + const_specs
    args = [z, lb, norm_g, *consts]
    if has_s0:
        in_specs.append(pl.BlockSpec((None, None, 2, N_HEADS, HEAD_DIM, HEAD_DIM),
                                     lambda b: (b, layer, 0, 0, 0, 0)))
        args.append(s0)
    out_specs = [pl.BlockSpec((seq_len, W_GROUP), lambda b: (b, 0))]
    out_shape = [jax.ShapeDtypeStruct((n_seq * seq_len, W_GROUP), F32)]
    if not has_s0:
        out_specs.append(pl.BlockSpec((None, 2, N_HEADS, HEAD_DIM, HEAD_DIM),
                                      lambda b: (b, 0, 0, 0, 0)))
        out_shape.append(jax.ShapeDtypeStruct((n_seq, 2, N_HEADS, HEAD_DIM, HEAD_DIM), F32))
    for w in casts:
        slab, cols = w.shape[1] // n_seq, w.shape[2]
        assert slab * n_seq == w.shape[1] and slab % CAST_ROWS == 0
        in_specs.append(pl.BlockSpec((None, slab, cols), lambda b: (layer, b, 0)))
        args.append(w)
        out_specs.append(pl.BlockSpec((slab, cols), lambda b: (b, 0)))
        out_shape.append(jax.ShapeDtypeStruct(w.shape[1:], BF16))
    kern = functools.partial(_hgrn_kernel, n_chunks=seq_len // HC, has_s0=has_s0,
                             n_cast=len(casts))
    return pl.pallas_call(
        kern,
        grid=(n_seq,),
        in_specs=in_specs,
        out_specs=out_specs,
        out_shape=out_shape,
        scratch_shapes=[
            pltpu.VMEM((2 * N_HEADS, HEAD_DIM, HEAD_DIM), F32),
            pltpu.VMEM((seq_len, W_GROUP), F32),
            pltpu.VMEM((seq_len, W_GROUP), F32),
        ],
        compiler_params=_params("arbitrary"),
        name="hgrn_lat" if has_s0 else "hgrn_ctx",
    )(*args)


def _mix_kernel(actx_ref, alat_ref, bu_ref, bv_ref, cb_ref, cc_ref, ch_ref, dx_ref,
                ccp_ref, chp_ref, dxp_ref, ccn_ref, chn_ref, dxn_ref,
                x_ref, g1_ref, sgun_ref, wsgu_ref, bsgu_ref, wconv_ref, wpool_ref, pscale_ref,
                wout_ref, o_ref, cat_scr):
    i = pl.program_id(0)
    is_ctx = i < N_TMIX_CTX
    lat_tile = jnp.maximum(i - N_TMIX_CTX, 0)
    j = jnp.where(is_ctx, 0, lat_tile % TMIX_PER_LAT)
    tiles_in_seq = jnp.where(is_ctx, 1, TMIX_PER_LAT)
    keep_prev = jnp.where(j == 0, 0.0, 1.0)
    keep_next = jnp.where(j == tiles_in_seq - 1, 0.0, 1.0)
    mod_row = jnp.where(is_ctx, 0, 1 + lat_tile // TMIX_PER_LAT)
    row = lax.broadcasted_iota(jnp.int32, (TMIX, 1), 0)

    def project(k):
        ks = slice(k * W_GROUP, (k + 1) * W_GROUP)
        return _dot(cat_scr[:, ks], wout_ref[ks, :])

    cat_scr[:, 0:W_GROUP] = jnp.where(is_ctx, actx_ref[...], alat_ref[...]).astype(BF16)
    o_ref[...] = project(0)

    u = jax.nn.gelu(bu_ref[...])
    vv = jax.nn.gelu(bv_ref[...])
    vn = (vv * lax.rsqrt(jnp.mean(vv * vv, axis=-1, keepdims=True) + EPS) * sgun_ref[...]).astype(BF16)
    for n in range(TMIX // CHUNK_B):
        rs = slice(n * CHUNK_B, (n + 1) * CHUNK_B)
        for h in range(N_HEADS):
            cs = slice(h * HEAD_DIM, (h + 1) * HEAD_DIM)
            mixed = _dot(wsgu_ref[h], vn[rs, cs]) + bsgu_ref[:, cs]
            cat_scr[rs, W_GROUP + h * HEAD_DIM:W_GROUP + (h + 1) * HEAD_DIM] = (u[rs, cs] * mixed).astype(BF16)
    o_ref[...] += project(1)

    p = cc_ref[...] * ch_ref[...]
    p_edge_prev = ccp_ref[POOL_HALO - 1:POOL_HALO, :] * chp_ref[POOL_HALO - 1:POOL_HALO, :] * keep_prev
    p_edge_next = ccn_ref[0:1, :] * chn_ref[0:1, :] * keep_next
    p_prev = jnp.where(row == 0, p_edge_prev, pltpu.roll(p, 1, 0))
    p_next = jnp.where(row == TMIX - 1, p_edge_next, pltpu.roll(p, TMIX - 1, 0))
    wc = wconv_ref[...]
    c_out = cb_ref[...] * (wc[0:1] * p_prev + wc[1:2] * p + wc[2:3] * p_next)
    cat_scr[:, 2 * W_GROUP:3 * W_GROUP] = c_out.astype(BF16)
    o_ref[...] += project(2)

    xd = dx_ref[...]
    ext = jnp.concatenate([dxp_ref[...] * keep_prev, xd, dxn_ref[...] * keep_next], axis=0)
    n_ext = TMIX + 2 * POOL_HALO
    back = lambda a, s: pltpu.roll(a, s, 0)
    ahead = lambda a, s: pltpu.roll(a, n_ext - s, 0)
    gd = W_GROUP // len(POOL_WINDOWS)
    sums = [back(ext, 1) + ext]
    for g, s in enumerate((1, 2, 4)):
        wider = sums[-1][:, gd:]
        sums.append(back(wider, s) + ahead(wider, s))
    pos = j * TMIX + row
    seq_len = tiles_in_seq * TMIX
    for g, win in enumerate(POOL_WINDOWS):
        cs = slice(g * gd, (g + 1) * gd)
        cnt = (jnp.minimum(pos + win // 2, seq_len) - jnp.maximum(pos - win // 2, 0)).astype(F32)
        mean = sums[g][POOL_HALO:POOL_HALO + TMIX, 0:gd] / cnt
        dg = _dot((mean - xd[:, cs]).astype(BF16), wpool_ref[g]) * pscale_ref[:, cs]
        cat_scr[:, 3 * W_GROUP + g * gd:3 * W_GROUP + (g + 1) * gd] = dg.astype(BF16)

    y = o_ref[...] + project(3)
    o_ref[...] = x_ref[...] + g1_ref[pl.ds(mod_row, 1), :] * y


def _mix(a_ctx, a_lat, z, x, mod, sgu_norm, w_sgu, b_sgu_full, w_conv, w_pool, pool_scale, w_out,
         layer):
    halo_blocks = TMIX // POOL_HALO
    last_halo = N_TOK // POOL_HALO - 1
    zcol = lambda k: pl.BlockSpec((TMIX, W_GROUP), lambda i: (i, k))
    zprev = lambda k: pl.BlockSpec((POOL_HALO, W_GROUP),
                                   lambda i: (jnp.maximum(i * halo_blocks - 1, 0), k))
    znext = lambda k: pl.BlockSpec((POOL_HALO, W_GROUP),
                                   lambda i: (jnp.minimum((i + 1) * halo_blocks, last_halo), k))
    full = lambda arr: pl.BlockSpec(arr.shape, lambda i, nd=arr.ndim: (0,) * nd)
    of_layer = lambda arr: pl.BlockSpec((None,) + arr.shape[1:],
                                        lambda i, nd=arr.ndim: (layer,) + (0,) * (nd - 1))
    return pl.pallas_call(
        _mix_kernel,
        grid=(N_TOK // TMIX,),
        in_specs=[
            pl.BlockSpec((TMIX, W_GROUP), lambda i: (jnp.minimum(i, N_TMIX_CTX - 1), 0)),
            pl.BlockSpec((TMIX, W_GROUP), lambda i: (jnp.maximum(i - N_TMIX_CTX, 0), 0)),
            zcol(5), zcol(6), zcol(7), zcol(8), zcol(9), zcol(10),
            zprev(8), zprev(9), zprev(10), znext(8), znext(9), znext(10),
            pl.BlockSpec((TMIX, D_MODEL), lambda i: (i, 0)),
            _mod_spec(layer, 2),
            full(sgu_norm), of_layer(w_sgu), full(b_sgu_full), full(w_conv), of_layer(w_pool),
            full(pool_scale), full(w_out),
        ],
        out_specs=pl.BlockSpec((TMIX, D_MODEL), lambda i: (i, 0)),
        out_shape=jax.ShapeDtypeStruct((N_TOK, D_MODEL), F32),
        scratch_shapes=[pltpu.VMEM((TMIX, D_MODEL), BF16)],
        compiler_params=_params("arbitrary"),
        name="mix",
    )(a_ctx, a_lat, z, z, z, z, z, z, z, z, z, z, z, z, x, mod, sgu_norm, w_sgu, b_sgu_full, w_conv, w_pool,
      pool_scale, w_out)


def _ffn_kernel(x_ref, g_ref, sh_ref, sc_ref, g2_ref, wv_ref, wg_ref, cv_ref, cg_ref, wd_ref,
                *rest, emit_next):
    if emit_next:
        gn_ref, shn_ref, scn_ref, o_ref, hn_ref, h_scr, uv_scr, ug_scr, act_scr = rest
    else:
        o_ref, h_scr, uv_scr, ug_scr, act_scr = rest
    i = pl.program_id(0)
    j = pl.program_id(1)
    r = _mod_row_of_big_tile(i)

    @pl.when(j == 0)
    def _():
        _norm_mod_rows(x_ref, h_scr, g_ref[...], sh_ref[pl.ds(r, 1), :], sc_ref[pl.ds(r, 1), :], TM)

        def zero_body(n, carry):
            rows = pl.ds(pl.multiple_of(n * NORM_ROWS, NORM_ROWS), NORM_ROWS)
            o_ref[rows, :] = jnp.zeros((NORM_ROWS, D_MODEL), F32)
            return carry
        lax.fori_loop(0, TM // NORM_ROWS, zero_body, 0)

    inner_keep = jnp.where(i < N_TM_CTX, 0.0, 1.0)

    def conv3_rows(u_scr, w, r0, cs):
        cur = u_scr[r0:r0 + FFN_RC, cs]
        if r0 == 0:
            prev = jnp.zeros((POOL_HALO, FFN_CT), F32)
        else:
            prev = u_scr[r0 - POOL_HALO:r0, cs]
            if r0 % SEQ == 0:
                prev = prev * inner_keep
        r1 = r0 + FFN_RC
        if r1 == TM:
            nxt = jnp.zeros((POOL_HALO, FFN_CT), F32)
        else:
            nxt = u_scr[r1:r1 + POOL_HALO, cs]
            if r1 % SEQ == 0:
                nxt = nxt * inner_keep
        ext = jnp.concatenate([prev, cur, nxt], axis=0)
        n_ext = FFN_RC + 2 * POOL_HALO
        before = pltpu.roll(ext, 1, 0)[POOL_HALO:POOL_HALO + FFN_RC]
        after = pltpu.roll(ext, n_ext - 1, 0)[POOL_HALO:POOL_HALO + FFN_RC]
        return w[0:1] * before + w[1:2] * cur + w[2:3] * after

    col_tiles = [slice(c * FFN_CT, (c + 1) * FFN_CT) for c in range(TF // FFN_CT)]
    for cs in col_tiles:
        uv_scr[:, cs] = _dot(h_scr[...], wv_ref[:, cs])
        ug_scr[:, cs] = _dot(h_scr[...], wg_ref[:, cs])
    for cs in col_tiles:
        cv = cv_ref[:, cs]
        cg = cg_ref[:, cs]
        for r0 in range(0, TM, FFN_RC):
            val = conv3_rows(uv_scr, cv, r0, cs)
            gate = conv3_rows(ug_scr, cg, r0, cs)
            act_scr[r0:r0 + FFN_RC, cs] = (val * _silu(gate)).astype(BF16)
    for c in range(D_MODEL // FFN_NOUT):
        ns = slice(c * FFN_NOUT, (c + 1) * FFN_NOUT)
        o_ref[:, ns] += _dot(act_scr[...], wd_ref[:, ns])

    @pl.when(j == pl.num_programs(1) - 1)
    def _():
        g2 = g2_ref[pl.ds(r, 1), :]

        def body(n, carry):
            rows = pl.ds(pl.multiple_of(n * NORM_ROWS, NORM_ROWS), NORM_ROWS)
            o_ref[rows, :] = x_ref[rows, :] + g2 * o_ref[rows, :]
            return carry
        lax.fori_loop(0, TM // NORM_ROWS, body, 0)
        if emit_next:
            _norm_mod_rows(o_ref, hn_ref, gn_ref[...], shn_ref[pl.ds(r, 1), :],
                           scn_ref[pl.ds(r, 1), :], TM)


def _ffn(x, gain, mod, w_up, w_conv, w_down, layer, next_gain=None):
    n_f = D_FF // TF
    emit_next = next_gain is not None
    tile = pl.BlockSpec((TM, D_MODEL), lambda i, j: (i, 0))
    next_specs, next_args = [], []
    out_specs = [tile]
    out_shape = [jax.ShapeDtypeStruct((N_TOK, D_MODEL), F32)]
    if emit_next:
        next_specs = [pl.BlockSpec((1, D_MODEL), lambda i, j: (0, 0)),
                      _mod_spec(layer + 1, 0), _mod_spec(layer + 1, 1)]
        next_args = [next_gain, mod, mod]
        out_specs.append(tile)
        out_shape.append(jax.ShapeDtypeStruct((N_TOK, D_MODEL), BF16))
    return pl.pallas_call(
        functools.partial(_ffn_kernel, emit_next=emit_next),
        grid=(N_TM, n_f),
        in_specs=[
            pl.BlockSpec((TM, D_MODEL), lambda i, j: (i, 0), pipeline_mode=pl.Buffered(1)),
            pl.BlockSpec((1, D_MODEL), lambda i, j: (0, 0)),
            _mod_spec(layer, 3),
            _mod_spec(layer, 4),
            _mod_spec(layer, 5),
            pl.BlockSpec((D_MODEL, TF), lambda i, j: (0, j)),
            pl.BlockSpec((D_MODEL, TF), lambda i, j: (0, j + n_f)),
            pl.BlockSpec((None, 3, TF), lambda i, j: (layer, 0, j)),
            pl.BlockSpec((None, 3, TF), lambda i, j: (layer, 0, j + n_f)),
            pl.BlockSpec((TF, D_MODEL), lambda i, j: (j, 0)),
        ] + next_specs,
        out_specs=out_specs,
        out_shape=out_shape,
        scratch_shapes=[pltpu.VMEM((TM, D_MODEL), BF16), pltpu.VMEM((TM, TF), F32),
                        pltpu.VMEM((TM, TF), F32), pltpu.VMEM((TM, TF), BF16)],
        compiler_params=_params("arbitrary", "arbitrary"),
        name="ffn",
    )(x, gain, mod, mod, mod, w_up, w_up, w_conv, w_conv, w_down, *next_args)


def _final_norm_kernel(x_ref, g_ref, o_ref):
    gain = g_ref[...]

    def body(n, carry):
        rows = pl.ds(pl.multiple_of(n * NORM_ROWS, NORM_ROWS), NORM_ROWS)
        x = x_ref[rows, :]
        o_ref[rows, :] = x * lax.rsqrt(jnp.mean(x * x, axis=-1, keepdims=True) + EPS) * gain
        return carry
    lax.fori_loop(0, TM // NORM_ROWS, body, 0)


def _final_norm(x, gain, tile0, n_tiles):
    return pl.pallas_call(
        _final_norm_kernel,
        grid=(n_tiles,),
        in_specs=[pl.BlockSpec((TM, D_MODEL), lambda i: (tile0 + i, 0)),
                  pl.BlockSpec((1, D_MODEL), lambda i: (0, 0))],
        out_specs=pl.BlockSpec((TM, D_MODEL), lambda i: (i, 0)),
        out_shape=jax.ShapeDtypeStruct((n_tiles * TM, D_MODEL), F32),
        compiler_params=_params("arbitrary"),
        name="final_norm",
    )(x, gain)


def _grid_pos_embed():
    rows = DEC_SEQ // GRID_W
    quarter = D_MODEL // 4
    freq = jnp.exp(-jnp.log(10000.0) * jnp.arange(quarter, dtype=F32) / quarter)[None, :]
    r = jnp.arange(rows, dtype=F32)[:, None] * freq
    col = jnp.arange(GRID_W, dtype=F32)[:, None] * freq
    rep = lambda a: jnp.repeat(a, GRID_W, axis=0)
    til = lambda a: jnp.tile(a, (rows, 1))
    return jnp.concatenate([rep(jnp.sin(r)), rep(jnp.cos(r)), til(jnp.sin(col)), til(jnp.cos(col))], -1)


def kernel(x_prompt, x_sample, c, state_hgrn, c_ctx, w_ada, b_ada, norm_mix, norm_ffn, w_in,
           lb_logits, hgrn_norm, sgu_norm, w_sgu, b_sgu, w_conv_c, w_pool, pool_scale, w_out,
           w_up, w_conv_ffn, w_down, norm_final):
    p = jax.nn.softmax(lb_logits.astype(F32), axis=0)
    lbs = jnp.cumsum(p, axis=0) - p[0:1]

    cvec = jnp.concatenate([c_ctx[None, :], c, jnp.zeros((MOD_ROWS - 1 - DEC_BATCH, D_MODEL), F32)], 0)
    mod = _adaln(cvec, w_ada, b_ada)

    consts = _hgrn_constants()
    w_sgu, w_pool = w_sgu.astype(BF16), w_pool.astype(BF16)
    states = []
    x = None
    for l in range(DEPTH):
        row = lambda a: a[l].reshape(1, -1)
        if l == 0:
            x, h = _prep(x_prompt.reshape(N_CTX, D_MODEL), x_sample.reshape(N_LAT, D_MODEL),
                         _grid_pos_embed(), row(norm_mix), mod, l)
        z = _inproj(h, w_in, l)
        a_ctx, st, w_up_l, w_down_l, w_out_l = _hgrn(
            z, lbs[l], row(hgrn_norm), consts, seq_len=SEQ, n_seq=BATCH, row_block0=0, layer=l,
            casts=(w_up, w_down, w_out))
        a_lat, = _hgrn(z, lbs[l], row(hgrn_norm), consts, seq_len=DEC_SEQ, n_seq=DEC_BATCH,
                       row_block0=N_CTX // DEC_SEQ, s0=state_hgrn, layer=l)
        b_sgu_full = jnp.repeat(b_sgu[l].T, HEAD_DIM, axis=1)
        x = _mix(a_ctx, a_lat, z, x, mod, row(sgu_norm), w_sgu, b_sgu_full, w_conv_c[l], w_pool,
                 row(pool_scale), w_out_l, l)
        if l + 1 < DEPTH:
            x, h = _ffn(x, row(norm_ffn), mod, w_up_l, w_conv_ffn, w_down_l, l,
                        next_gain=norm_mix[l + 1].reshape(1, -1))
        else:
            x, = _ffn(x, row(norm_ffn), mod, w_up_l, w_conv_ffn, w_down_l, l)
        states.append(st)

    gain = norm_final.reshape(1, -1)
    y_prompt = _final_norm(x, gain, 0, N_TM_CTX).reshape(BATCH, SEQ, D_MODEL)
    y_sample = _final_norm(x, gain, N_TM_CTX, N_TM - N_TM_CTX).reshape(DEC_BATCH, DEC_SEQ, D_MODEL)
    return (y_prompt, y_sample, jnp.stack(states, axis=1))
```

```python
import functools

import numpy as np
import jax
import jax.numpy as jnp
from jax import lax
from jax.experimental import pallas as pl
from jax.experimental.pallas import tpu as pltpu

F32 = jnp.float32
BF16 = jnp.bfloat16

D_MODEL = 2048
BATCH, SEQ = 16, 256
DEC_BATCH, DEC_SEQ = 2, 1024
DEPTH = 2
GRID_W = 64
HEAD_DIM = 128
W_GROUP = D_MODEL // 4
N_HEADS = W_GROUP // HEAD_DIM
POOL_WINDOWS = (2, 4, 8, 16)
POOL_HALO = 8
CHUNK_B = 128
D_FF = 5632
D_IN = 11 * W_GROUP
EPS = 1e-6

N_CTX = BATCH * SEQ
N_LAT = DEC_BATCH * DEC_SEQ
N_TOK = N_CTX + N_LAT
MOD_ROWS = 8

TM = 1024
N_TM = N_TOK // TM
N_TM_CTX = N_CTX // TM
TM_IN = 2048
TN_IN = 512
TF = 512
FFN_CT = 256
FFN_RC = 64
FFN_NOUT = 512
TMIX = 256
N_TMIX_CTX = N_CTX // TMIX
TMIX_PER_LAT = DEC_SEQ // TMIX
TP = 512
N_TP_CTX = N_CTX // TP
TP_PER_LAT = DEC_SEQ // TP
TN_ADA = 1024
NORM_ROWS = 128
CAST_ROWS = 32

HC = 64
HC_LEVELS = 6
HC_FINE_LEVELS = 3
VMEM_LIMIT = 56 * 1024 * 1024


def _silu(x):
    return x * jax.nn.sigmoid(x)


def _dot(a, b):
    return jnp.dot(a, b, preferred_element_type=F32)


def _dot_nt(a, b):
    return lax.dot_general(a, b, (((1,), (1,)), ((), ())), preferred_element_type=F32)


def _params(*sem):
    return pltpu.CompilerParams(dimension_semantics=sem, vmem_limit_bytes=VMEM_LIMIT)


def _adaln_kernel(c_ref, w_ref, b_ref, o_ref):
    s = _silu(c_ref[...]).astype(BF16)
    o_ref[...] = _dot(s, w_ref[...].astype(BF16)) + b_ref[...]


def _adaln(cvec, w_ada, b_ada):
    n_out = w_ada.shape[-1]
    return pl.pallas_call(
        _adaln_kernel,
        grid=(DEPTH, n_out // TN_ADA),
        in_specs=[
            pl.BlockSpec((MOD_ROWS, D_MODEL), lambda l, j: (0, 0)),
            pl.BlockSpec((None, D_MODEL, TN_ADA), lambda l, j: (l, 0, j)),
            pl.BlockSpec((None, 1, TN_ADA), lambda l, j: (l, 0, j)),
        ],
        out_specs=pl.BlockSpec((None, MOD_ROWS, TN_ADA), lambda l, j: (l, 0, j)),
        out_shape=jax.ShapeDtypeStruct((DEPTH, MOD_ROWS, n_out), F32),
        compiler_params=_params("arbitrary", "arbitrary"),
        name="adaln",
    )(cvec, w_ada, b_ada.reshape(DEPTH, 1, n_out))


def _norm_mod_rows(x_ref, h_ref, gain, shift, scale, n_rows):
    gain_eff = gain * (1.0 + scale)

    def body(n, carry):
        r0 = pl.multiple_of(n * NORM_ROWS, NORM_ROWS)
        x = x_ref[pl.ds(r0, NORM_ROWS), :]
        ms = jnp.mean(x * x, axis=-1, keepdims=True)
        h_ref[pl.ds(r0, NORM_ROWS), :] = (x * lax.rsqrt(ms + EPS) * gain_eff + shift).astype(BF16)
        return carry
    lax.fori_loop(0, n_rows // NORM_ROWS, body, 0)


def _mod_row_of_big_tile(i):
    return jnp.maximum(i - (N_TM_CTX - 1), 0)


def _mod_spec(layer, k):
    return pl.BlockSpec((None, MOD_ROWS, D_MODEL), lambda *_: (layer, 0, k))


def _prep_kernel(xp_ref, xs_ref, pos_ref, g_ref, sh_ref, sc_ref, x_ref, h_ref):
    i = pl.program_id(0)
    lat_tile = jnp.maximum(i - N_TP_CTX, 0)
    r = jnp.where(i < N_TP_CTX, 0, 1 + lat_tile // TP_PER_LAT)

    @pl.when(i < N_TP_CTX)
    def _():
        def body(n, carry):
            rows = pl.ds(pl.multiple_of(n * NORM_ROWS, NORM_ROWS), NORM_ROWS)
            x_ref[rows, :] = xp_ref[rows, :]
            return carry
        lax.fori_loop(0, TP // NORM_ROWS, body, 0)

    @pl.when(i >= N_TP_CTX)
    def _():
        def body(n, carry):
            rows = pl.ds(pl.multiple_of(n * NORM_ROWS, NORM_ROWS), NORM_ROWS)
            x_ref[rows, :] = xs_ref[rows, :] + pos_ref[rows, :]
            return carry
        lax.fori_loop(0, TP // NORM_ROWS, body, 0)

    _norm_mod_rows(x_ref, h_ref, g_ref[...], sh_ref[pl.ds(r, 1), :], sc_ref[pl.ds(r, 1), :], TP)


def _prep(x_prompt, x_sample, pos, gain, mod, layer):
    lat = lambda i: jnp.maximum(i - N_TP_CTX, 0)
    return pl.pallas_call(
        _prep_kernel,
        grid=(N_TOK // TP,),
        in_specs=[
            pl.BlockSpec((TP, D_MODEL), lambda i: (jnp.minimum(i, N_TP_CTX - 1), 0)),
            pl.BlockSpec((TP, D_MODEL), lambda i: (lat(i), 0)),
            pl.BlockSpec((TP, D_MODEL), lambda i: (lat(i) % TP_PER_LAT, 0)),
            pl.BlockSpec((1, D_MODEL), lambda i: (0, 0)),
            _mod_spec(layer, 0),
            _mod_spec(layer, 1),
        ],
        out_specs=[pl.BlockSpec((TP, D_MODEL), lambda i: (i, 0)),
                   pl.BlockSpec((TP, D_MODEL), lambda i: (i, 0))],
        out_shape=[jax.ShapeDtypeStruct((N_TOK, D_MODEL), F32),
                   jax.ShapeDtypeStruct((N_TOK, D_MODEL), BF16)],
        compiler_params=_params("arbitrary"),
        name="prep",
    )(x_prompt, x_sample, pos, gain, mod, mod)


def _inproj_kernel(h_ref, w_ref, z_ref):
    z_ref[...] = _dot(h_ref[...], w_ref[...].astype(BF16))


def _inproj(h, w_in, layer):
    return pl.pallas_call(
        _inproj_kernel,
        grid=(N_TOK // TM_IN, D_IN // TN_IN),
        in_specs=[
            pl.BlockSpec((TM_IN, D_MODEL), lambda i, j: (i, 0)),
            pl.BlockSpec((None, D_MODEL, TN_IN), lambda i, j: (layer, 0, j)),
        ],
        out_specs=pl.BlockSpec((TM_IN, TN_IN), lambda i, j: (i, j)),
        out_shape=jax.ShapeDtypeStruct((N_TOK, D_IN), F32),
        compiler_params=_params("arbitrary", "arbitrary"),
        name="inproj",
    )(h, w_in)


def _hgrn_constants():
    t = np.arange(HC)
    run = [(t[None, :] <= t[:, None]), (t[None, :] >= t[:, None])]
    g = np.zeros((2, HC_FINE_LEVELS + 1, HC, HC), np.float32)
    uq = np.ones((2, HC_LEVELS + 1, HC), np.float32)
    bm = np.zeros((HC_LEVELS + 1, HC, HC), np.float32)
    for d in range(2):
        g[d, 0] = run[d]
        for lev in range(HC_LEVELS):
            m = 1 << lev
            base = (t // (2 * m)) * (2 * m)
            if lev < HC_FINE_LEVELS:
                g[d, lev + 1] = run[d][base + m - 1 + d]
            upper = (t % (2 * m)) >= m
            uq[d, lev] = upper if d == 0 else ~upper
    uk = 1.0 - uq
    uk[:, HC_LEVELS] = 1.0
    for lev in range(HC_LEVELS):
        m = 1 << lev
        bm[lev] = (t[:, None] // (2 * m)) == (t[None, :] // (2 * m))
    bm[HC_LEVELS] = np.eye(HC)
    pair = bm[None] * uq[:, :, :, None] * uk[:, :, None, :]
    g = g.reshape(2, (HC_FINE_LEVELS + 1) * HC, HC)
    g3 = np.concatenate([g, g, g], axis=-1)
    lanes = lambda a: np.ascontiguousarray(np.broadcast_to(a[..., None], a.shape + (HEAD_DIM,)))
    return (jnp.asarray(g3, BF16), jnp.asarray(lanes(uq[:, :HC_FINE_LEVELS]), F32),
            jnp.asarray(pair, F32))


def _hgrn_chunk(d, r0, z_ref, lb_ref, g3_ref, uq_ref, pair_ref, s_scr, o_scr):
    rows = pl.ds(r0, HC)
    lb = lb_ref[d:d + 1, :]
    f = lb + (1.0 - lb) * jax.nn.sigmoid(z_ref[rows, (2 + d) * W_GROUP:(3 + d) * W_GROUP])
    kk_all = 1.0 - f
    lf = jnp.log(f)

    l1 = lf.astype(BF16)
    r1 = lf - l1.astype(F32)
    l2 = r1.astype(BF16)
    l3 = (r1 - l2.astype(F32)).astype(BF16)
    sums = _dot(g3_ref[d], jnp.concatenate([l1, l2, l3], axis=0))

    for h in range(N_HEADS):
        cs = slice(h * HEAD_DIM, (h + 1) * HEAD_DIM)
        q = z_ref[rows, cs]
        vb = z_ref[rows, W_GROUP + h * HEAD_DIM:W_GROUP + (h + 1) * HEAD_DIM].astype(BF16)
        kk = kk_all[:, cs]
        cum = sums[0:HC, cs]

        lo, hi = (kk, q) if d == 0 else (q, kk)
        att = _dot_nt(q.astype(BF16), kk.astype(BF16)) * pair_ref[d, HC_LEVELS]
        for lev in range(HC_LEVELS):
            m = 1 << lev
            if lev < HC_FINE_LEVELS:
                bound = sums[(lev + 1) * HC:(lev + 2) * HC, cs]
                rows_qk = jnp.where(uq_ref[d, lev] > 0.5, q, kk)
            else:
                bound = jnp.concatenate(
                    [jnp.broadcast_to(cum[b0 + m - 1 + d:b0 + m + d, :], (2 * m, HEAD_DIM))
                     for b0 in range(0, HC, 2 * m)], axis=0)
                rows_qk = jnp.concatenate(
                    [part for b0 in range(0, HC, 2 * m)
                     for part in (lo[b0:b0 + m], hi[b0 + m:b0 + 2 * m])], axis=0)
            decayed = (rows_qk * jnp.exp(-jnp.abs(cum - bound))).astype(BF16)
            att = att + _dot_nt(decayed, decayed) * pair_ref[d, lev]

        far = cum[HC - 1:HC, :] if d == 0 else cum[0:1, :]
        s_old = s_scr[d * N_HEADS + h]
        q_in = (q * jnp.exp(cum)).astype(BF16)
        o_scr[rows, cs] = _dot(jnp.concatenate([q_in, att.astype(BF16)], axis=1),
                               jnp.concatenate([s_old.astype(BF16), vb], axis=0))

        k_out = kk * jnp.exp(-jnp.abs(cum - far))
        k_ext = jnp.concatenate([k_out, jnp.broadcast_to(jnp.exp(far), (8, HEAD_DIM))], axis=0)
        k_ext_t = k_ext.T
        s_scr[d * N_HEADS + h] = (k_ext_t[:, HC:HC + 1] * s_old
                                  + _dot(k_ext_t[:, 0:HC].astype(BF16), vb))


def _hgrn_kernel(*refs, n_chunks, has_s0, n_cast):
    z_ref, lb_ref, ng_ref, g3_ref, uq_ref, pair_ref = refs[:6]
    k = 6
    s0_ref = st_ref = None
    if has_s0:
        s0_ref = refs[k]
        k += 1
    cast_src = refs[k:k + n_cast]
    k += n_cast
    a_ref = refs[k]
    k += 1
    if not has_s0:
        st_ref = refs[k]
        k += 1
    cast_dst = refs[k:k + n_cast]
    k += n_cast
    s_scr, of_scr, ob_scr = refs[k:k + 3]

    for src, dst in zip(cast_src, cast_dst):
        for r0 in range(0, src.shape[0], CAST_ROWS):
            dst[r0:r0 + CAST_ROWS, :] = src[r0:r0 + CAST_ROWS, :].astype(BF16)

    for d in range(2):
        for h in range(N_HEADS):
            s_scr[d * N_HEADS + h] = (s0_ref[d, h] if has_s0
                                      else jnp.zeros((HEAD_DIM, HEAD_DIM), F32))

    def scan_body(n, carry):
        for d, o_scr in ((0, of_scr), (1, ob_scr)):
            r0 = pl.multiple_of((n if d == 0 else n_chunks - 1 - n) * HC, HC)
            _hgrn_chunk(d, r0, z_ref, lb_ref, g3_ref, uq_ref, pair_ref, s_scr, o_scr)
        return carry
    lax.fori_loop(0, n_chunks, scan_body, 0, unroll=4)

    def out_body(n, carry):
        rows = pl.ds(pl.multiple_of(n * HC, HC), HC)
        o = of_scr[rows, :] + ob_scr[rows, :]
        parts = []
        for h in range(N_HEADS):
            oh = o[:, h * HEAD_DIM:(h + 1) * HEAD_DIM]
            parts.append(oh * lax.rsqrt(jnp.mean(oh * oh, axis=-1, keepdims=True) + EPS))
        gate = z_ref[rows, 4 * W_GROUP:5 * W_GROUP]
        a_ref[rows, :] = jnp.concatenate(parts, axis=-1) * ng_ref[...] * _silu(gate)
        return carry
    lax.fori_loop(0, n_chunks, out_body, 0)

    if st_ref is not None:
        for d in range(2):
            for h in range(N_HEADS):
                st_ref[d, h] = s_scr[d * N_HEADS + h]


def _hgrn(z, lb, norm_g, consts, *, seq_len, n_seq, row_block0, s0=None, layer=0, casts=()):
    has_s0 = s0 is not None
    const_specs = [pl.BlockSpec(c.shape, lambda b, nd=c.ndim: (0,) * nd) for c in consts]
    in_specs = [
        pl.BlockSpec((seq_len, 5 * W_GROUP), lambda b: (row_block0 + b, 0)),
        pl.BlockSpec((2, W_GROUP), lambda b: (0, 0)),
        pl.BlockSpec((1, W_GROUP), lambda b: (0, 0)),
    ] + const_specs
    args = [z, lb, norm_g, *consts]
    if has_s0:
        in_specs.append(pl.BlockSpec((None, None, 2, N_HEADS, HEAD_DIM, HEAD_DIM),
                                     lambda b: (b, layer, 0, 0, 0, 0)))
        args.append(s0)
    out_specs = [pl.BlockSpec((seq_len, W_GROUP), lambda b: (b, 0))]
    out_shape = [jax.ShapeDtypeStruct((n_seq * seq_len, W_GROUP), F32)]
    if not has_s0:
        out_specs.append(pl.BlockSpec((None, 2, N_HEADS, HEAD_DIM, HEAD_DIM),
                                      lambda b: (b, 0, 0, 0, 0)))
        out_shape.append(jax.ShapeDtypeStruct((n_seq, 2, N_HEADS, HEAD_DIM, HEAD_DIM), F32))
    for w in casts:
        slab, cols = w.shape[1] // n_seq, w.shape[2]
        assert slab * n_seq == w.shape[1] and slab % CAST_ROWS == 0
        in_specs.append(pl.BlockSpec((None, slab, cols), lambda b: (layer, b, 0)))
        args.append(w)
        out_specs.append(pl.BlockSpec((slab, cols), lambda b: (b, 0)))
        out_shape.append(jax.ShapeDtypeStruct(w.shape[1:], BF16))
    kern = functools.partial(_hgrn_kernel, n_chunks=seq_len // HC, has_s0=has_s0,
                             n_cast=len(casts))
    return pl.pallas_call(
        kern,
        grid=(n_seq,),
        in_specs=in_specs,
        out_specs=out_specs,
        out_shape=out_shape,
        scratch_shapes=[
            pltpu.VMEM((2 * N_HEADS, HEAD_DIM, HEAD_DIM), F32),
            pltpu.VMEM((seq_len, W_GROUP), F32),
            pltpu.VMEM((seq_len, W_GROUP), F32),
        ],
        compiler_params=_params("arbitrary"),
        name="hgrn_lat" if has_s0 else "hgrn_ctx",
    )(*args)


def _mix_kernel(actx_ref, alat_ref, bu_ref, bv_ref, cb_ref, cc_ref, ch_ref, dx_ref,
                ccp_ref, chp_ref, dxp_ref, ccn_ref, chn_ref, dxn_ref,
                x_ref, g1_ref, sgun_ref, wsgu_ref, bsgu_ref, wconv_ref, wpool_ref, pscale_ref,
                wout_ref, o_ref, cat_scr):
    i = pl.program_id(0)
    is_ctx = i < N_TMIX_CTX
    lat_tile = jnp.maximum(i - N_TMIX_CTX, 0)
    j = jnp.where(is_ctx, 0, lat_tile % TMIX_PER_LAT)
    tiles_in_seq = jnp.where(is_ctx, 1, TMIX_PER_LAT)
    keep_prev = jnp.where(j == 0, 0.0, 1.0)
    keep_next = jnp.where(j == tiles_in_seq - 1, 0.0, 1.0)
    mod_row = jnp.where(is_ctx, 0, 1 + lat_tile // TMIX_PER_LAT)
    row = lax.broadcasted_iota(jnp.int32, (TMIX, 1), 0)

    def project(k):
        ks = slice(k * W_GROUP, (k + 1) * W_GROUP)
        return _dot(cat_scr[:, ks], wout_ref[ks, :])

    cat_scr[:, 0:W_GROUP] = jnp.where(is_ctx, actx_ref[...], alat_ref[...]).astype(BF16)
    o_ref[...] = project(0)

    u = jax.nn.gelu(bu_ref[...])
    vv = jax.nn.gelu(bv_ref[...])
    vn = (vv * lax.rsqrt(jnp.mean(vv * vv, axis=-1, keepdims=True) + EPS) * sgun_ref[...]).astype(BF16)
    for n in range(TMIX // CHUNK_B):
        rs = slice(n * CHUNK_B, (n + 1) * CHUNK_B)
        for h in range(N_HEADS):
            cs = slice(h * HEAD_DIM, (h + 1) * HEAD_DIM)
            mixed = _dot(wsgu_ref[h], vn[rs, cs]) + bsgu_ref[:, cs]
            cat_scr[rs, W_GROUP + h * HEAD_DIM:W_GROUP + (h + 1) * HEAD_DIM] = (u[rs, cs] * mixed).astype(BF16)
    o_ref[...] += project(1)

    p = cc_ref[...] * ch_ref[...]
    p_edge_prev = ccp_ref[POOL_HALO - 1:POOL_HALO, :] * chp_ref[POOL_HALO - 1:POOL_HALO, :] * keep_prev
    p_edge_next = ccn_ref[0:1, :] * chn_ref[0:1, :] * keep_next
    p_prev = jnp.where(row == 0, p_edge_prev, pltpu.roll(p, 1, 0))
    p_next = jnp.where(row == TMIX - 1, p_edge_next, pltpu.roll(p, TMIX - 1, 0))
    wc = wconv_ref[...]
    c_out = cb_ref[...] * (wc[0:1] * p_prev + wc[1:2] * p + wc[2:3] * p_next)
    cat_scr[:, 2 * W_GROUP:3 * W_GROUP] = c_out.astype(BF16)
    o_ref[...] += project(2)

    xd = dx_ref[...]
    ext = jnp.concatenate([dxp_ref[...] * keep_prev, xd, dxn_ref[...] * keep_next], axis=0)
    n_ext = TMIX + 2 * POOL_HALO
    back = lambda a, s: pltpu.roll(a, s, 0)
    ahead = lambda a, s: pltpu.roll(a, n_ext - s, 0)
    gd = W_GROUP // len(POOL_WINDOWS)
    sums = [back(ext, 1) + ext]
    for g, s in enumerate((1, 2, 4)):
        wider = sums[-1][:, gd:]
        sums.append(back(wider, s) + ahead(wider, s))
    pos = j * TMIX + row
    seq_len = tiles_in_seq * TMIX
    for g, win in enumerate(POOL_WINDOWS):
        cs = slice(g * gd, (g + 1) * gd)
        cnt = (jnp.minimum(pos + win // 2, seq_len) - jnp.maximum(pos - win // 2, 0)).astype(F32)
        mean = sums[g][POOL_HALO:POOL_HALO + TMIX, 0:gd] / cnt
        dg = _dot((mean - xd[:, cs]).astype(BF16), wpool_ref[g]) * pscale_ref[:, cs]
        cat_scr[:, 3 * W_GROUP + g * gd:3 * W_GROUP + (g + 1) * gd] = dg.astype(BF16)

    y = o_ref[...] + project(3)
    o_ref[...] = x_ref[...] + g1_ref[pl.ds(mod_row, 1), :] * y


def _mix(a_ctx, a_lat, z, x, mod, sgu_norm, w_sgu, b_sgu_full, w_conv, w_pool, pool_scale, w_out,
         layer):
    halo_blocks = TMIX // POOL_HALO
    last_halo = N_TOK // POOL_HALO - 1
    zcol = lambda k: pl.BlockSpec((TMIX, W_GROUP), lambda i: (i, k))
    zprev = lambda k: pl.BlockSpec((POOL_HALO, W_GROUP),
                                   lambda i: (jnp.maximum(i * halo_blocks - 1, 0), k))
    znext = lambda k: pl.BlockSpec((POOL_HALO, W_GROUP),
                                   lambda i: (jnp.minimum((i + 1) * halo_blocks, last_halo), k))
    full = lambda arr: pl.BlockSpec(arr.shape, lambda i, nd=arr.ndim: (0,) * nd)
    of_layer = lambda arr: pl.BlockSpec((None,) + arr.shape[1:],
                                        lambda i, nd=arr.ndim: (layer,) + (0,) * (nd - 1))
    return pl.pallas_call(
        _mix_kernel,
        grid=(N_TOK // TMIX,),
        in_specs=[
            pl.BlockSpec((TMIX, W_GROUP), lambda i: (jnp.minimum(i, N_TMIX_CTX - 1), 0)),
            pl.BlockSpec((TMIX, W_GROUP), lambda i: (jnp.maximum(i - N_TMIX_CTX, 0), 0)),
            zcol(5), zcol(6), zcol(7), zcol(8), zcol(9), zcol(10),
            zprev(8), zprev(9), zprev(10), znext(8), znext(9), znext(10),
            pl.BlockSpec((TMIX, D_MODEL), lambda i: (i, 0)),
            _mod_spec(layer, 2),
            full(sgu_norm), of_layer(w_sgu), full(b_sgu_full), full(w_conv), of_layer(w_pool),
            full(pool_scale), full(w_out),
        ],
        out_specs=pl.BlockSpec((TMIX, D_MODEL), lambda i: (i, 0)),
        out_shape=jax.ShapeDtypeStruct((N_TOK, D_MODEL), F32),
        scratch_shapes=[pltpu.VMEM((TMIX, D_MODEL), BF16)],
        compiler_params=_params("arbitrary"),
        name="mix",
    )(a_ctx, a_lat, z, z, z, z, z, z, z, z, z, z, z, z, x, mod, sgu_norm, w_sgu, b_sgu_full, w_conv, w_pool,
      pool_scale, w_out)


def _ffn_kernel(x_ref, g_ref, sh_ref, sc_ref, g2_ref, wv_ref, wg_ref, cv_ref, cg_ref, wd_ref,
                *rest, emit_next):
    if emit_next:
        gn_ref, shn_ref, scn_ref, o_ref, hn_ref, h_scr, uv_scr, ug_scr, act_scr = rest
    else:
        o_ref, h_scr, uv_scr, ug_scr, act_scr = rest
    i = pl.program_id(0)
    j = pl.program_id(1)
    r = _mod_row_of_big_tile(i)

    @pl.when(j == 0)
    def _():
        _norm_mod_rows(x_ref, h_scr, g_ref[...], sh_ref[pl.ds(r, 1), :], sc_ref[pl.ds(r, 1), :], TM)

        def zero_body(n, carry):
            rows = pl.ds(pl.multiple_of(n * NORM_ROWS, NORM_ROWS), NORM_ROWS)
            o_ref[rows, :] = jnp.zeros((NORM_ROWS, D_MODEL), F32)
            return carry
        lax.fori_loop(0, TM // NORM_ROWS, zero_body, 0)

    inner_keep = jnp.where(i < N_TM_CTX, 0.0, 1.0)

    def conv3_rows(u_scr, w, r0, cs):
        cur = u_scr[r0:r0 + FFN_RC, cs]
        if r0 == 0:
            prev = jnp.zeros((POOL_HALO, FFN_CT), F32)
        else:
            prev = u_scr[r0 - POOL_HALO:r0, cs]
            if r0 % SEQ == 0:
                prev = prev * inner_keep
        r1 = r0 + FFN_RC
        if r1 == TM:
            nxt = jnp.zeros((POOL_HALO, FFN_CT), F32)
        else:
            nxt = u_scr[r1:r1 + POOL_HALO, cs]
            if r1 % SEQ == 0:
                nxt = nxt * inner_keep
        ext = jnp.concatenate([prev, cur, nxt], axis=0)
        n_ext = FFN_RC + 2 * POOL_HALO
        before = pltpu.roll(ext, 1, 0)[POOL_HALO:POOL_HALO + FFN_RC]
        after = pltpu.roll(ext, n_ext - 1, 0)[POOL_HALO:POOL_HALO + FFN_RC]
        return w[0:1] * before + w[1:2] * cur + w[2:3] * after

    col_tiles = [slice(c * FFN_CT, (c + 1) * FFN_CT) for c in range(TF // FFN_CT)]
    for cs in col_tiles:
        uv_scr[:, cs] = _dot(h_scr[...], wv_ref[:, cs])
        ug_scr[:, cs] = _dot(h_scr[...], wg_ref[:, cs])
    for cs in col_tiles:
        cv = cv_ref[:, cs]
        cg = cg_ref[:, cs]
        for r0 in range(0, TM, FFN_RC):
            val = conv3_rows(uv_scr, cv, r0, cs)
            gate = conv3_rows(ug_scr, cg, r0, cs)
            act_scr[r0:r0 + FFN_RC, cs] = (val * _silu(gate)).astype(BF16)
    for c in range(D_MODEL // FFN_NOUT):
        ns = slice(c * FFN_NOUT, (c + 1) * FFN_NOUT)
        o_ref[:, ns] += _dot(act_scr[...], wd_ref[:, ns])

    @pl.when(j == pl.num_programs(1) - 1)
    def _():
        g2 = g2_ref[pl.ds(r, 1), :]

        def body(n, carry):
            rows = pl.ds(pl.multiple_of(n * NORM_ROWS, NORM_ROWS), NORM_ROWS)
            o_ref[rows, :] = x_ref[rows, :] + g2 * o_ref[rows, :]
            return carry
        lax.fori_loop(0, TM // NORM_ROWS, body, 0)
        if emit_next:
            _norm_mod_rows(o_ref, hn_ref, gn_ref[...], shn_ref[pl.ds(r, 1), :],
                           scn_ref[pl.ds(r, 1), :], TM)


def _ffn(x, gain, mod, w_up, w_conv, w_down, layer, next_gain=None):
    n_f = D_FF // TF
    emit_next = next_gain is not None
    tile = pl.BlockSpec((TM, D_MODEL), lambda i, j: (i, 0))
    next_specs, next_args = [], []
    out_specs = [tile]
    out_shape = [jax.ShapeDtypeStruct((N_TOK, D_MODEL), F32)]
    if emit_next:
        next_specs = [pl.BlockSpec((1, D_MODEL), lambda i, j: (0, 0)),
                      _mod_spec(layer + 1, 0), _mod_spec(layer + 1, 1)]
        next_args = [next_gain, mod, mod]
        out_specs.append(tile)
        out_shape.append(jax.ShapeDtypeStruct((N_TOK, D_MODEL), BF16))
    return pl.pallas_call(
        functools.partial(_ffn_kernel, emit_next=emit_next),
        grid=(N_TM, n_f),
        in_specs=[
            pl.BlockSpec((TM, D_MODEL), lambda i, j: (i, 0),
                         pipeline_mode=pl.Buffered(1 if emit_next else 2)),
            pl.BlockSpec((1, D_MODEL), lambda i, j: (0, 0)),
            _mod_spec(layer, 3),
            _mod_spec(layer, 4),
            _mod_spec(layer, 5),
            pl.BlockSpec((D_MODEL, TF), lambda i, j: (0, j)),
            pl.BlockSpec((D_MODEL, TF), lambda i, j: (0, j + n_f)),
            pl.BlockSpec((None, 3, TF), lambda i, j: (layer, 0, j)),
            pl.BlockSpec((None, 3, TF), lambda i, j: (layer, 0, j + n_f)),
            pl.BlockSpec((TF, D_MODEL), lambda i, j: (j, 0)),
        ] + next_specs,
        out_specs=out_specs,
        out_shape=out_shape,
        scratch_shapes=[pltpu.VMEM((TM, D_MODEL), BF16), pltpu.VMEM((TM, TF), F32),
                        pltpu.VMEM((TM, TF), F32), pltpu.VMEM((TM, TF), BF16)],
        compiler_params=_params("arbitrary", "arbitrary"),
        name="ffn",
    )(x, gain, mod, mod, mod, w_up, w_up, w_conv, w_conv, w_down, *next_args)


def _final_norm_kernel(x_ref, g_ref, o_ref):
    gain = g_ref[...]

    def body(n, carry):
        rows = pl.ds(pl.multiple_of(n * NORM_ROWS, NORM_ROWS), NORM_ROWS)
        x = x_ref[rows, :]
        o_ref[rows, :] = x * lax.rsqrt(jnp.mean(x * x, axis=-1, keepdims=True) + EPS) * gain
        return carry
    lax.fori_loop(0, TM // NORM_ROWS, body, 0)


def _final_norm(x, gain, tile0, n_tiles):
    return pl.pallas_call(
        _final_norm_kernel,
        grid=(n_tiles,),
        in_specs=[pl.BlockSpec((TM, D_MODEL), lambda i: (tile0 + i, 0)),
                  pl.BlockSpec((1, D_MODEL), lambda i: (0, 0))],
        out_specs=pl.BlockSpec((TM, D_MODEL), lambda i: (i, 0)),
        out_shape=jax.ShapeDtypeStruct((n_tiles * TM, D_MODEL), F32),
        compiler_params=_params("arbitrary"),
        name="final_norm",
    )(x, gain)


def _grid_pos_embed():
    rows = DEC_SEQ // GRID_W
    quarter = D_MODEL // 4
    freq = jnp.exp(-jnp.log(10000.0) * jnp.arange(quarter, dtype=F32) / quarter)[None, :]
    r = jnp.arange(rows, dtype=F32)[:, None] * freq
    col = jnp.arange(GRID_W, dtype=F32)[:, None] * freq
    rep = lambda a: jnp.repeat(a, GRID_W, axis=0)
    til = lambda a: jnp.tile(a, (rows, 1))
    return jnp.concatenate([rep(jnp.sin(r)), rep(jnp.cos(r)), til(jnp.sin(col)), til(jnp.cos(col))], -1)


def kernel(x_prompt, x_sample, c, state_hgrn, c_ctx, w_ada, b_ada, norm_mix, norm_ffn, w_in,
           lb_logits, hgrn_norm, sgu_norm, w_sgu, b_sgu, w_conv_c, w_pool, pool_scale, w_out,
           w_up, w_conv_ffn, w_down, norm_final):
    p = jax.nn.softmax(lb_logits.astype(F32), axis=0)
    lbs = jnp.cumsum(p, axis=0) - p[0:1]

    cvec = jnp.concatenate([c_ctx[None, :], c, jnp.zeros((MOD_ROWS - 1 - DEC_BATCH, D_MODEL), F32)], 0)
    mod = _adaln(cvec, w_ada, b_ada)

    consts = _hgrn_constants()
    w_sgu, w_pool = w_sgu.astype(BF16), w_pool.astype(BF16)
    states = []
    x = None
    for l in range(DEPTH):
        row = lambda a: a[l].reshape(1, -1)
        if l == 0:
            x, h = _prep(x_prompt.reshape(N_CTX, D_MODEL), x_sample.reshape(N_LAT, D_MODEL),
                         _grid_pos_embed(), row(norm_mix), mod, l)
        z = _inproj(h, w_in, l)
        a_ctx, st, w_up_l, w_down_l, w_out_l = _hgrn(
            z, lbs[l], row(hgrn_norm), consts, seq_len=SEQ, n_seq=BATCH, row_block0=0, layer=l,
            casts=(w_up, w_down, w_out))
        a_lat, = _hgrn(z, lbs[l], row(hgrn_norm), consts, seq_len=DEC_SEQ, n_seq=DEC_BATCH,
                       row_block0=N_CTX // DEC_SEQ, s0=state_hgrn, layer=l)
        b_sgu_full = jnp.repeat(b_sgu[l].T, HEAD_DIM, axis=1)
        x = _mix(a_ctx, a_lat, z, x, mod, row(sgu_norm), w_sgu, b_sgu_full, w_conv_c[l], w_pool,
                 row(pool_scale), w_out_l, l)
        if l + 1 < DEPTH:
            x, h = _ffn(x, row(norm_ffn), mod, w_up_l, w_conv_ffn, w_down_l, l,
                        next_gain=norm_mix[l + 1].reshape(1, -1))
        else:
            x, = _ffn(x, row(norm_ffn), mod, w_up_l, w_conv_ffn, w_down_l, l)
        states.append(st)

    gain = norm_final.reshape(1, -1)
    y_prompt = _final_norm(x, gain, 0, N_TM_CTX).reshape(BATCH, SEQ, D_MODEL)
    y_sample = _final_norm(x, gain, N_TM_CTX, N_TM - N_TM_CTX).reshape(DEC_BATCH, DEC_SEQ, D_MODEL)
    return (y_prompt, y_sample, jnp.stack(states, axis=1))
```

```python
import functools

import numpy as np
import jax
import jax.numpy as jnp
from jax import lax
from jax.experimental import pallas as pl
from jax.experimental.pallas import tpu as pltpu

F32 = jnp.float32
BF16 = jnp.bfloat16

D_MODEL = 2048
BATCH, SEQ = 16, 256
DEC_BATCH, DEC_SEQ = 2, 1024
DEPTH = 2
GRID_W = 64
HEAD_DIM = 128
W_GROUP = D_MODEL // 4
N_HEADS = W_GROUP // HEAD_DIM
POOL_WINDOWS = (2, 4, 8, 16)
POOL_HALO = 8
CHUNK_B = 128
D_FF = 5632
D_IN = 11 * W_GROUP
EPS = 1e-6

N_CTX = BATCH * SEQ
N_LAT = DEC_BATCH * DEC_SEQ
N_TOK = N_CTX + N_LAT
MOD_ROWS = 8

TM = 1024
N_TM = N_TOK // TM
N_TM_CTX = N_CTX // TM
TM_IN = 2048
TN_IN = 512
TF = 512
FFN_CT = 256
FFN_RC = 64
FFN_NOUT = 512
TMIX = 256
N_TMIX_CTX = N_CTX // TMIX
TMIX_PER_LAT = DEC_SEQ // TMIX
TP = 512
N_TP_CTX = N_CTX // TP
TP_PER_LAT = DEC_SEQ // TP
TN_ADA = 1024
NORM_ROWS = 128
CAST_ROWS = 32

HC = 64
HC_LEVELS = 6
HC_FINE_LEVELS = 3
VMEM_LIMIT = 56 * 1024 * 1024


def _silu(x):
    return x * jax.nn.sigmoid(x)


def _dot(a, b):
    return jnp.dot(a, b, preferred_element_type=F32)


def _dot_nt(a, b):
    return lax.dot_general(a, b, (((1,), (1,)), ((), ())), preferred_element_type=F32)


def _params(*sem):
    return pltpu.CompilerParams(dimension_semantics=sem, vmem_limit_bytes=VMEM_LIMIT)


def _adaln_kernel(c_ref, w_ref, b_ref, o_ref):
    s = _silu(c_ref[...]).astype(BF16)
    o_ref[...] = _dot(s, w_ref[...].astype(BF16)) + b_ref[...]


def _adaln(cvec, w_ada, b_ada):
    n_out = w_ada.shape[-1]
    return pl.pallas_call(
        _adaln_kernel,
        grid=(DEPTH, n_out // TN_ADA),
        in_specs=[
            pl.BlockSpec((MOD_ROWS, D_MODEL), lambda l, j: (0, 0)),
            pl.BlockSpec((None, D_MODEL, TN_ADA), lambda l, j: (l, 0, j)),
            pl.BlockSpec((None, 1, TN_ADA), lambda l, j: (l, 0, j)),
        ],
        out_specs=pl.BlockSpec((None, MOD_ROWS, TN_ADA), lambda l, j: (l, 0, j)),
        out_shape=jax.ShapeDtypeStruct((DEPTH, MOD_ROWS, n_out), F32),
        compiler_params=_params("arbitrary", "arbitrary"),
        name="adaln",
    )(cvec, w_ada, b_ada.reshape(DEPTH, 1, n_out))


def _norm_mod_rows(x_ref, h_ref, gain, shift, scale, n_rows):
    gain_eff = gain * (1.0 + scale)

    def body(n, carry):
        r0 = pl.multiple_of(n * NORM_ROWS, NORM_ROWS)
        x = x_ref[pl.ds(r0, NORM_ROWS), :]
        ms = jnp.mean(x * x, axis=-1, keepdims=True)
        h_ref[pl.ds(r0, NORM_ROWS), :] = (x * lax.rsqrt(ms + EPS) * gain_eff + shift).astype(BF16)
        return carry
    lax.fori_loop(0, n_rows // NORM_ROWS, body, 0)


def _mod_row_of_big_tile(i):
    return jnp.maximum(i - (N_TM_CTX - 1), 0)


def _mod_spec(layer, k):
    return pl.BlockSpec((None, MOD_ROWS, D_MODEL), lambda *_: (layer, 0, k))


def _prep_kernel(xp_ref, xs_ref, pos_ref, g_ref, sh_ref, sc_ref, x_ref, h_ref):
    i = pl.program_id(0)
    lat_tile = jnp.maximum(i - N_TP_CTX, 0)
    r = jnp.where(i < N_TP_CTX, 0, 1 + lat_tile // TP_PER_LAT)

    @pl.when(i < N_TP_CTX)
    def _():
        def body(n, carry):
            rows = pl.ds(pl.multiple_of(n * NORM_ROWS, NORM_ROWS), NORM_ROWS)
            x_ref[rows, :] = xp_ref[rows, :]
            return carry
        lax.fori_loop(0, TP // NORM_ROWS, body, 0)

    @pl.when(i >= N_TP_CTX)
    def _():
        def body(n, carry):
            rows = pl.ds(pl.multiple_of(n * NORM_ROWS, NORM_ROWS), NORM_ROWS)
            x_ref[rows, :] = xs_ref[rows, :] + pos_ref[rows, :]
            return carry
        lax.fori_loop(0, TP // NORM_ROWS, body, 0)

    _norm_mod_rows(x_ref, h_ref, g_ref[...], sh_ref[pl.ds(r, 1), :], sc_ref[pl.ds(r, 1), :], TP)


def _prep(x_prompt, x_sample, pos, gain, mod, layer):
    lat = lambda i: jnp.maximum(i - N_TP_CTX, 0)
    return pl.pallas_call(
        _prep_kernel,
        grid=(N_TOK // TP,),
        in_specs=[
            pl.BlockSpec((TP, D_MODEL), lambda i: (jnp.minimum(i, N_TP_CTX - 1), 0)),
            pl.BlockSpec((TP, D_MODEL), lambda i: (lat(i), 0)),
            pl.BlockSpec((TP, D_MODEL), lambda i: (lat(i) % TP_PER_LAT, 0)),
            pl.BlockSpec((1, D_MODEL), lambda i: (0, 0)),
            _mod_spec(layer, 0),
            _mod_spec(layer, 1),
        ],
        out_specs=[pl.BlockSpec((TP, D_MODEL), lambda i: (i, 0)),
                   pl.BlockSpec((TP, D_MODEL), lambda i: (i, 0))],
        out_shape=[jax.ShapeDtypeStruct((N_TOK, D_MODEL), F32),
                   jax.ShapeDtypeStruct((N_TOK, D_MODEL), BF16)],
        compiler_params=_params("arbitrary"),
        name="prep",
    )(x_prompt, x_sample, pos, gain, mod, mod)


def _inproj_kernel(h_ref, w_ref, z_ref):
    z_ref[...] = _dot(h_ref[...], w_ref[...].astype(BF16))


def _inproj(h, w_in, layer):
    return pl.pallas_call(
        _inproj_kernel,
        grid=(N_TOK // TM_IN, D_IN // TN_IN),
        in_specs=[
            pl.BlockSpec((TM_IN, D_MODEL), lambda i, j: (i, 0)),
            pl.BlockSpec((None, D_MODEL, TN_IN), lambda i, j: (layer, 0, j)),
        ],
        out_specs=pl.BlockSpec((TM_IN, TN_IN), lambda i, j: (i, j)),
        out_shape=jax.ShapeDtypeStruct((N_TOK, D_IN), F32),
        compiler_params=_params("arbitrary", "arbitrary"),
        name="inproj",
    )(h, w_in)


def _hgrn_constants():
    t = np.arange(HC)
    run = [(t[None, :] <= t[:, None]), (t[None, :] >= t[:, None])]
    g = np.zeros((2, HC_FINE_LEVELS + 1, HC, HC), np.float32)
    uq = np.ones((2, HC_LEVELS + 1, HC), np.float32)
    bm = np.zeros((HC_LEVELS + 1, HC, HC), np.float32)
    for d in range(2):
        g[d, 0] = run[d]
        for lev in range(HC_LEVELS):
            m = 1 << lev
            base = (t // (2 * m)) * (2 * m)
            if lev < HC_FINE_LEVELS:
                g[d, lev + 1] = run[d][base + m - 1 + d]
            upper = (t % (2 * m)) >= m
            uq[d, lev] = upper if d == 0 else ~upper
    uk = 1.0 - uq
    uk[:, HC_LEVELS] = 1.0
    for lev in range(HC_LEVELS):
        m = 1 << lev
        bm[lev] = (t[:, None] // (2 * m)) == (t[None, :] // (2 * m))
    bm[HC_LEVELS] = np.eye(HC)
    pair = bm[None] * uq[:, :, :, None] * uk[:, :, None, :]
    g = g.reshape(2, (HC_FINE_LEVELS + 1) * HC, HC)
    g3 = np.concatenate([g, g, g], axis=-1)
    lanes = lambda a: np.ascontiguousarray(np.broadcast_to(a[..., None], a.shape + (HEAD_DIM,)))
    return (jnp.asarray(g3, BF16), jnp.asarray(lanes(uq[:, :HC_FINE_LEVELS]), F32),
            jnp.asarray(pair, F32))


def _hgrn_chunk(d, r0, z_ref, lb_ref, g3_ref, uq_ref, pair_ref, s_scr, o_scr):
    rows = pl.ds(r0, HC)
    lb = lb_ref[d:d + 1, :]
    f = lb + (1.0 - lb) * jax.nn.sigmoid(z_ref[rows, (2 + d) * W_GROUP:(3 + d) * W_GROUP])
    kk_all = 1.0 - f
    lf = jnp.log(f)

    l1 = lf.astype(BF16)
    r1 = lf - l1.astype(F32)
    l2 = r1.astype(BF16)
    l3 = (r1 - l2.astype(F32)).astype(BF16)
    sums = _dot(g3_ref[d], jnp.concatenate([l1, l2, l3], axis=0))

    for h in range(N_HEADS):
        cs = slice(h * HEAD_DIM, (h + 1) * HEAD_DIM)
        q = z_ref[rows, cs]
        vb = z_ref[rows, W_GROUP + h * HEAD_DIM:W_GROUP + (h + 1) * HEAD_DIM].astype(BF16)
        kk = kk_all[:, cs]
        cum = sums[0:HC, cs]

        lo, hi = (kk, q) if d == 0 else (q, kk)
        att = _dot_nt(q.astype(BF16), kk.astype(BF16)) * pair_ref[d, HC_LEVELS]
        for lev in range(HC_LEVELS):
            m = 1 << lev
            if lev < HC_FINE_LEVELS:
                bound = sums[(lev + 1) * HC:(lev + 2) * HC, cs]
                rows_qk = jnp.where(uq_ref[d, lev] > 0.5, q, kk)
            else:
                bound = jnp.concatenate(
                    [jnp.broadcast_to(cum[b0 + m - 1 + d:b0 + m + d, :], (2 * m, HEAD_DIM))
                     for b0 in range(0, HC, 2 * m)], axis=0)
                rows_qk = jnp.concatenate(
                    [part for b0 in range(0, HC, 2 * m)
                     for part in (lo[b0:b0 + m], hi[b0 + m:b0 + 2 * m])], axis=0)
            decayed = (rows_qk * jnp.exp(-jnp.abs(cum - bound))).astype(BF16)
            att = att + _dot_nt(decayed, decayed) * pair_ref[d, lev]

        far = cum[HC - 1:HC, :] if d == 0 else cum[0:1, :]
        s_old = s_scr[d * N_HEADS + h]
        q_in = (q * jnp.exp(cum)).astype(BF16)
        o_scr[rows, cs] = _dot(jnp.concatenate([q_in, att.astype(BF16)], axis=1),
                               jnp.concatenate([s_old.astype(BF16), vb], axis=0))

        k_out = kk * jnp.exp(-jnp.abs(cum - far))
        k_ext = jnp.concatenate([k_out, jnp.broadcast_to(jnp.exp(far), (8, HEAD_DIM))], axis=0)
        k_ext_t = k_ext.T
        s_scr[d * N_HEADS + h] = (k_ext_t[:, HC:HC + 1] * s_old
                                  + _dot(k_ext_t[:, 0:HC].astype(BF16), vb))


def _hgrn_kernel(*refs, n_chunks, has_s0, n_cast):
    z_ref, lb_ref, ng_ref, g3_ref, uq_ref, pair_ref = refs[:6]
    k = 6
    s0_ref = st_ref = None
    if has_s0:
        s0_ref = refs[k]
        k += 1
    cast_src = refs[k:k + n_cast]
    k += n_cast
    a_ref = refs[k]
    k += 1
    if not has_s0:
        st_ref = refs[k]
        k += 1
    cast_dst = refs[k:k + n_cast]
    k += n_cast
    s_scr, of_scr, ob_scr = refs[k:k + 3]

    for src, dst in zip(cast_src, cast_dst):
        for r0 in range(0, src.shape[0], CAST_ROWS):
            dst[r0:r0 + CAST_ROWS, :] = src[r0:r0 + CAST_ROWS, :].astype(BF16)

    for d in range(2):
        for h in range(N_HEADS):
            s_scr[d * N_HEADS + h] = (s0_ref[d, h] if has_s0
                                      else jnp.zeros((HEAD_DIM, HEAD_DIM), F32))

    def scan_body(n, carry):
        for d, o_scr in ((0, of_scr), (1, ob_scr)):
            r0 = pl.multiple_of((n if d == 0 else n_chunks - 1 - n) * HC, HC)
            _hgrn_chunk(d, r0, z_ref, lb_ref, g3_ref, uq_ref, pair_ref, s_scr, o_scr)
        return carry
    lax.fori_loop(0, n_chunks, scan_body, 0, unroll=4)

    def out_body(n, carry):
        rows = pl.ds(pl.multiple_of(n * HC, HC), HC)
        o = of_scr[rows, :] + ob_scr[rows, :]
        parts = []
        for h in range(N_HEADS):
            oh = o[:, h * HEAD_DIM:(h + 1) * HEAD_DIM]
            parts.append(oh * lax.rsqrt(jnp.mean(oh * oh, axis=-1, keepdims=True) + EPS))
        gate = z_ref[rows, 4 * W_GROUP:5 * W_GROUP]
        a_ref[rows, :] = jnp.concatenate(parts, axis=-1) * ng_ref[...] * _silu(gate)
        return carry
    lax.fori_loop(0, n_chunks, out_body, 0)

    if st_ref is not None:
        for d in range(2):
            for h in range(N_HEADS):
                st_ref[d, h] = s_scr[d * N_HEADS + h]


def _hgrn(z, lb, norm_g, consts, *, seq_len, n_seq, row_block0, s0=None, layer=0, casts=()):
    has_s0 = s0 is not None
    const_specs = [pl.BlockSpec(c.shape, lambda b, nd=c.ndim: (0,) * nd) for c in consts]
    in_specs = [
        pl.BlockSpec((seq_len, 5 * W_GROUP), lambda b: (row_block0 + b, 0)),
        pl.BlockSpec((2, W_GROUP), lambda b: (0, 0)),
        pl.BlockSpec((1, W_GROUP), lambda b: (0, 0)),
    ] + const_specs
    args = [z, lb, norm_g, *consts]
    if has_s0:
        in_specs.append(pl.BlockSpec((None, None, 2, N_HEADS, HEAD_DIM, HEAD_DIM),
                                     lambda b: (b, layer, 0, 0, 0, 0)))
        args.append(s0)
    out_specs = [pl.BlockSpec((seq_len, W_GROUP), lambda b: (b, 0))]
    out_shape = [jax.ShapeDtypeStruct((n_seq * seq_len, W_GROUP), F32)]
    if not has_s0:
        out_specs.append(pl.BlockSpec((None, 2, N_HEADS, HEAD_DIM, HEAD_DIM),
                                      lambda b: (b, 0, 0, 0, 0)))
        out_shape.append(jax.ShapeDtypeStruct((n_seq, 2, N_HEADS, HEAD_DIM, HEAD_DIM), F32))
    for w in casts:
        slab, cols = w.shape[1] // n_seq, w.shape[2]
        assert slab * n_seq == w.shape[1] and slab % CAST_ROWS == 0
        in_specs.append(pl.BlockSpec((None, slab, cols), lambda b: (layer, b, 0)))
        args.append(w)
        out_specs.append(pl.BlockSpec((slab, cols), lambda b: (b, 0)))
        out_shape.append(jax.ShapeDtypeStruct(w.shape[1:], BF16))
    kern = functools.partial(_hgrn_kernel, n_chunks=seq_len // HC, has_s0=has_s0,
                             n_cast=len(casts))
    return pl.pallas_call(
        kern,
        grid=(n_seq,),
        in_specs=in_specs,
        out_specs=out_specs,
        out_shape=out_shape,
        scratch_shapes=[
            pltpu.VMEM((2 * N_HEADS, HEAD_DIM, HEAD_DIM), F32),
            pltpu.VMEM((seq_len, W_GROUP), F32),
            pltpu.VMEM((seq_len, W_GROUP), F32),
        ],
        compiler_params=_params("arbitrary"),
        name="hgrn_lat" if has_s0 else "hgrn_ctx",
    )(*args)


def _mix_kernel(actx_ref, alat_ref, bu_ref, bv_ref, cb_ref, cc_ref, ch_ref, dx_ref,
                ccp_ref, chp_ref, dxp_ref, ccn_ref, chn_ref, dxn_ref,
                x_ref, g1_ref, sgun_ref, wsgu_ref, bsgu_ref, wconv_ref, wpool_ref, pscale_ref,
                wout_ref, o_ref, cat_scr):
    i = pl.program_id(0)
    is_ctx = i < N_TMIX_CTX
    lat_tile = jnp.maximum(i - N_TMIX_CTX, 0)
    j = jnp.where(is_ctx, 0, lat_tile % TMIX_PER_LAT)
    tiles_in_seq = jnp.where(is_ctx, 1, TMIX_PER_LAT)
    keep_prev = jnp.where(j == 0, 0.0, 1.0)
    keep_next = jnp.where(j == tiles_in_seq - 1, 0.0, 1.0)
    mod_row = jnp.where(is_ctx, 0, 1 + lat_tile // TMIX_PER_LAT)
    row = lax.broadcasted_iota(jnp.int32, (TMIX, 1), 0)

    def project(k):
        ks = slice(k * W_GROUP, (k + 1) * W_GROUP)
        return _dot(cat_scr[:, ks], wout_ref[ks, :])

    cat_scr[:, 0:W_GROUP] = jnp.where(is_ctx, actx_ref[...], alat_ref[...]).astype(BF16)
    o_ref[...] = project(0)

    u = jax.nn.gelu(bu_ref[...])
    vv = jax.nn.gelu(bv_ref[...])
    vn = (vv * lax.rsqrt(jnp.mean(vv * vv, axis=-1, keepdims=True) + EPS) * sgun_ref[...]).astype(BF16)
    for n in range(TMIX // CHUNK_B):
        rs = slice(n * CHUNK_B, (n + 1) * CHUNK_B)
        for h in range(N_HEADS):
            cs = slice(h * HEAD_DIM, (h + 1) * HEAD_DIM)
            mixed = _dot(wsgu_ref[h], vn[rs, cs]) + bsgu_ref[:, cs]
            cat_scr[rs, W_GROUP + h * HEAD_DIM:W_GROUP + (h + 1) * HEAD_DIM] = (u[rs, cs] * mixed).astype(BF16)
    o_ref[...] += project(1)

    p = cc_ref[...] * ch_ref[...]
    p_edge_prev = ccp_ref[POOL_HALO - 1:POOL_HALO, :] * chp_ref[POOL_HALO - 1:POOL_HALO, :] * keep_prev
    p_edge_next = ccn_ref[0:1, :] * chn_ref[0:1, :] * keep_next
    p_prev = jnp.where(row == 0, p_edge_prev, pltpu.roll(p, 1, 0))
    p_next = jnp.where(row == TMIX - 1, p_edge_next, pltpu.roll(p, TMIX - 1, 0))
    wc = wconv_ref[...]
    c_out = cb_ref[...] * (wc[0:1] * p_prev + wc[1:2] * p + wc[2:3] * p_next)
    cat_scr[:, 2 * W_GROUP:3 * W_GROUP] = c_out.astype(BF16)
    o_ref[...] += project(2)

    xd = dx_ref[...]
    ext = jnp.concatenate([dxp_ref[...] * keep_prev, xd, dxn_ref[...] * keep_next], axis=0)
    n_ext = TMIX + 2 * POOL_HALO
    back = lambda a, s: pltpu.roll(a, s, 0)
    ahead = lambda a, s: pltpu.roll(a, n_ext - s, 0)
    gd = W_GROUP // len(POOL_WINDOWS)
    sums = [back(ext, 1) + ext]
    for g, s in enumerate((1, 2, 4)):
        wider = sums[-1][:, gd:]
        sums.append(back(wider, s) + ahead(wider, s))
    pos = j * TMIX + row
    seq_len = tiles_in_seq * TMIX
    for g, win in enumerate(POOL_WINDOWS):
        cs = slice(g * gd, (g + 1) * gd)
        cnt = (jnp.minimum(pos + win // 2, seq_len) - jnp.maximum(pos - win // 2, 0)).astype(F32)
        mean = sums[g][POOL_HALO:POOL_HALO + TMIX, 0:gd] / cnt
        dg = _dot((mean - xd[:, cs]).astype(BF16), wpool_ref[g]) * pscale_ref[:, cs]
        cat_scr[:, 3 * W_GROUP + g * gd:3 * W_GROUP + (g + 1) * gd] = dg.astype(BF16)

    y = o_ref[...] + project(3)
    o_ref[...] = x_ref[...] + g1_ref[pl.ds(mod_row, 1), :] * y


def _mix(a_ctx, a_lat, z, x, mod, sgu_norm, w_sgu, b_sgu_full, w_conv, w_pool, pool_scale, w_out,
         layer):
    halo_blocks = TMIX // POOL_HALO
    last_halo = N_TOK // POOL_HALO - 1
    zcol = lambda k: pl.BlockSpec((TMIX, W_GROUP), lambda i: (i, k))
    zprev = lambda k: pl.BlockSpec((POOL_HALO, W_GROUP),
                                   lambda i: (jnp.maximum(i * halo_blocks - 1, 0), k))
    znext = lambda k: pl.BlockSpec((POOL_HALO, W_GROUP),
                                   lambda i: (jnp.minimum((i + 1) * halo_blocks, last_halo), k))
    full = lambda arr: pl.BlockSpec(arr.shape, lambda i, nd=arr.ndim: (0,) * nd)
    of_layer = lambda arr: pl.BlockSpec((None,) + arr.shape[1:],
                                        lambda i, nd=arr.ndim: (layer,) + (0,) * (nd - 1))
    return pl.pallas_call(
        _mix_kernel,
        grid=(N_TOK // TMIX,),
        in_specs=[
            pl.BlockSpec((TMIX, W_GROUP), lambda i: (jnp.minimum(i, N_TMIX_CTX - 1), 0)),
            pl.BlockSpec((TMIX, W_GROUP), lambda i: (jnp.maximum(i - N_TMIX_CTX, 0), 0)),
            zcol(5), zcol(6), zcol(7), zcol(8), zcol(9), zcol(10),
            zprev(8), zprev(9), zprev(10), znext(8), znext(9), znext(10),
            pl.BlockSpec((TMIX, D_MODEL), lambda i: (i, 0)),
            _mod_spec(layer, 2),
            full(sgu_norm), of_layer(w_sgu), full(b_sgu_full), full(w_conv), of_layer(w_pool),
            full(pool_scale), full(w_out),
        ],
        out_specs=pl.BlockSpec((TMIX, D_MODEL), lambda i: (i, 0)),
        out_shape=jax.ShapeDtypeStruct((N_TOK, D_MODEL), F32),
        scratch_shapes=[pltpu.VMEM((TMIX, D_MODEL), BF16)],
        compiler_params=_params("arbitrary"),
        name="mix",
    )(a_ctx, a_lat, z, z, z, z, z, z, z, z, z, z, z, z, x, mod, sgu_norm, w_sgu, b_sgu_full, w_conv, w_pool,
      pool_scale, w_out)


def _ffn_kernel(x_ref, g_ref, sh_ref, sc_ref, g2_ref, wv_ref, wg_ref, cv_ref, cg_ref, wd_ref,
                *rest, emit_next):
    if emit_next:
        gn_ref, shn_ref, scn_ref, o_ref, hn_ref, uv_scr, ug_scr, act_scr = rest
        h_scr = hn_ref
    else:
        o_ref, h_scr, uv_scr, ug_scr, act_scr = rest
    i = pl.program_id(0)
    j = pl.program_id(1)
    r = _mod_row_of_big_tile(i)

    @pl.when(j == 0)
    def _():
        _norm_mod_rows(x_ref, h_scr, g_ref[...], sh_ref[pl.ds(r, 1), :], sc_ref[pl.ds(r, 1), :], TM)

        def zero_body(n, carry):
            rows = pl.ds(pl.multiple_of(n * NORM_ROWS, NORM_ROWS), NORM_ROWS)
            o_ref[rows, :] = jnp.zeros((NORM_ROWS, D_MODEL), F32)
            return carry
        lax.fori_loop(0, TM // NORM_ROWS, zero_body, 0)

    inner_keep = jnp.where(i < N_TM_CTX, 0.0, 1.0)

    def conv3_rows(u_scr, w, r0, cs):
        cur = u_scr[r0:r0 + FFN_RC, cs]
        if r0 == 0:
            prev = jnp.zeros((POOL_HALO, FFN_CT), F32)
        else:
            prev = u_scr[r0 - POOL_HALO:r0, cs]
            if r0 % SEQ == 0:
                prev = prev * inner_keep
        r1 = r0 + FFN_RC
        if r1 == TM:
            nxt = jnp.zeros((POOL_HALO, FFN_CT), F32)
        else:
            nxt = u_scr[r1:r1 + POOL_HALO, cs]
            if r1 % SEQ == 0:
                nxt = nxt * inner_keep
        ext = jnp.concatenate([prev, cur, nxt], axis=0)
        n_ext = FFN_RC + 2 * POOL_HALO
        before = pltpu.roll(ext, 1, 0)[POOL_HALO:POOL_HALO + FFN_RC]
        after = pltpu.roll(ext, n_ext - 1, 0)[POOL_HALO:POOL_HALO + FFN_RC]
        return w[0:1] * before + w[1:2] * cur + w[2:3] * after

    col_tiles = [slice(c * FFN_CT, (c + 1) * FFN_CT) for c in range(TF // FFN_CT)]
    for cs in col_tiles:
        uv_scr[:, cs] = _dot(h_scr[...], wv_ref[:, cs])
        ug_scr[:, cs] = _dot(h_scr[...], wg_ref[:, cs])
    for cs in col_tiles:
        cv = cv_ref[:, cs]
        cg = cg_ref[:, cs]
        for r0 in range(0, TM, FFN_RC):
            val = conv3_rows(uv_scr, cv, r0, cs)
            gate = conv3_rows(ug_scr, cg, r0, cs)
            act_scr[r0:r0 + FFN_RC, cs] = (val * _silu(gate)).astype(BF16)
    for c in range(D_MODEL // FFN_NOUT):
        ns = slice(c * FFN_NOUT, (c + 1) * FFN_NOUT)
        o_ref[:, ns] += _dot(act_scr[...], wd_ref[:, ns])

    @pl.when(j == pl.num_programs(1) - 1)
    def _():
        g2 = g2_ref[pl.ds(r, 1), :]

        def body(n, carry):
            rows = pl.ds(pl.multiple_of(n * NORM_ROWS, NORM_ROWS), NORM_ROWS)
            o_ref[rows, :] = x_ref[rows, :] + g2 * o_ref[rows, :]
            return carry
        lax.fori_loop(0, TM // NORM_ROWS, body, 0)
        if emit_next:
            _norm_mod_rows(o_ref, hn_ref, gn_ref[...], shn_ref[pl.ds(r, 1), :],
                           scn_ref[pl.ds(r, 1), :], TM)


def _ffn(x, gain, mod, w_up, w_conv, w_down, layer, next_gain=None):
    n_f = D_FF // TF
    emit_next = next_gain is not None
    tile = pl.BlockSpec((TM, D_MODEL), lambda i, j: (i, 0))
    next_specs, next_args = [], []
    out_specs = [tile]
    out_shape = [jax.ShapeDtypeStruct((N_TOK, D_MODEL), F32)]
    if emit_next:
        next_specs = [pl.BlockSpec((1, D_MODEL), lambda i, j: (0, 0)),
                      _mod_spec(layer + 1, 0), _mod_spec(layer + 1, 1)]
        next_args = [next_gain, mod, mod]
        out_specs.append(tile)
        out_shape.append(jax.ShapeDtypeStruct((N_TOK, D_MODEL), BF16))
    return pl.pallas_call(
        functools.partial(_ffn_kernel, emit_next=emit_next),
        grid=(N_TM, n_f),
        in_specs=[
            pl.BlockSpec((TM, D_MODEL), lambda i, j: (i, 0)),
            pl.BlockSpec((1, D_MODEL), lambda i, j: (0, 0)),
            _mod_spec(layer, 3),
            _mod_spec(layer, 4),
            _mod_spec(layer, 5),
            pl.BlockSpec((D_MODEL, TF), lambda i, j: (0, j)),
            pl.BlockSpec((D_MODEL, TF), lambda i, j: (0, j + n_f)),
            pl.BlockSpec((None, 3, TF), lambda i, j: (layer, 0, j)),
            pl.BlockSpec((None, 3, TF), lambda i, j: (layer, 0, j + n_f)),
            pl.BlockSpec((TF, D_MODEL), lambda i, j: (j, 0)),
        ] + next_specs,
        out_specs=out_specs,
        out_shape=out_shape,
        scratch_shapes=([] if emit_next else [pltpu.VMEM((TM, D_MODEL), BF16)])
        + [pltpu.VMEM((TM, TF), F32), pltpu.VMEM((TM, TF), F32), pltpu.VMEM((TM, TF), BF16)],
        compiler_params=_params("arbitrary", "arbitrary"),
        name="ffn",
    )(x, gain, mod, mod, mod, w_up, w_up, w_conv, w_conv, w_down, *next_args)


def _final_norm_kernel(x_ref, g_ref, o_ref):
    gain = g_ref[...]

    def body(n, carry):
        rows = pl.ds(pl.multiple_of(n * NORM_ROWS, NORM_ROWS), NORM_ROWS)
        x = x_ref[rows, :]
        o_ref[rows, :] = x * lax.rsqrt(jnp.mean(x * x, axis=-1, keepdims=True) + EPS) * gain
        return carry
    lax.fori_loop(0, TM // NORM_ROWS, body, 0)


def _final_norm(x, gain, tile0, n_tiles):
    return pl.pallas_call(
        _final_norm_kernel,
        grid=(n_tiles,),
        in_specs=[pl.BlockSpec((TM, D_MODEL), lambda i: (tile0 + i, 0)),
                  pl.BlockSpec((1, D_MODEL), lambda i: (0, 0))],
        out_specs=pl.BlockSpec((TM, D_MODEL), lambda i: (i, 0)),
        out_shape=jax.ShapeDtypeStruct((n_tiles * TM, D_MODEL), F32),
        compiler_params=_params("arbitrary"),
        name="final_norm",
    )(x, gain)


def _grid_pos_embed():
    rows = DEC_SEQ // GRID_W
    quarter = D_MODEL // 4
    freq = jnp.exp(-jnp.log(10000.0) * jnp.arange(quarter, dtype=F32) / quarter)[None, :]
    r = jnp.arange(rows, dtype=F32)[:, None] * freq
    col = jnp.arange(GRID_W, dtype=F32)[:, None] * freq
    rep = lambda a: jnp.repeat(a, GRID_W, axis=0)
    til = lambda a: jnp.tile(a, (rows, 1))
    return jnp.concatenate([rep(jnp.sin(r)), rep(jnp.cos(r)), til(jnp.sin(col)), til(jnp.cos(col))], -1)


def kernel(x_prompt, x_sample, c, state_hgrn, c_ctx, w_ada, b_ada, norm_mix, norm_ffn, w_in,
           lb_logits, hgrn_norm, sgu_norm, w_sgu, b_sgu, w_conv_c, w_pool, pool_scale, w_out,
           w_up, w_conv_ffn, w_down, norm_final):
    p = jax.nn.softmax(lb_logits.astype(F32), axis=0)
    lbs = jnp.cumsum(p, axis=0) - p[0:1]

    cvec = jnp.concatenate([c_ctx[None, :], c, jnp.zeros((MOD_ROWS - 1 - DEC_BATCH, D_MODEL), F32)], 0)
    mod = _adaln(cvec, w_ada, b_ada)

    consts = _hgrn_constants()
    w_sgu, w_pool = w_sgu.astype(BF16), w_pool.astype(BF16)
    states = []
    x = None
    for l in range(DEPTH):
        row = lambda a: a[l].reshape(1, -1)
        if l == 0:
            x, h = _prep(x_prompt.reshape(N_CTX, D_MODEL), x_sample.reshape(N_LAT, D_MODEL),
                         _grid_pos_embed(), row(norm_mix), mod, l)
        z = _inproj(h, w_in, l)
        a_ctx, st, w_up_l, w_down_l, w_out_l = _hgrn(
            z, lbs[l], row(hgrn_norm), consts, seq_len=SEQ, n_seq=BATCH, row_block0=0, layer=l,
            casts=(w_up, w_down, w_out))
        a_lat, = _hgrn(z, lbs[l], row(hgrn_norm), consts, seq_len=DEC_SEQ, n_seq=DEC_BATCH,
                       row_block0=N_CTX // DEC_SEQ, s0=state_hgrn, layer=l)
        b_sgu_full = jnp.repeat(b_sgu[l].T, HEAD_DIM, axis=1)
        x = _mix(a_ctx, a_lat, z, x, mod, row(sgu_norm), w_sgu, b_sgu_full, w_conv_c[l], w_pool,
                 row(pool_scale), w_out_l, l)
        if l + 1 < DEPTH:
            x, h = _ffn(x, row(norm_ffn), mod, w_up_l, w_conv_ffn, w_down_l, l,
                        next_gain=norm_mix[l + 1].reshape(1, -1))
        else:
            x, = _ffn(x, row(norm_ffn), mod, w_up_l, w_conv_ffn, w_down_l, l)
        states.append(st)

    gain = norm_final.reshape(1, -1)
    y_prompt = _final_norm(x, gain, 0, N_TM_CTX).reshape(BATCH, SEQ, D_MODEL)
    y_sample = _final_norm(x, gain, N_TM_CTX, N_TM - N_TM_CTX).reshape(DEC_BATCH, DEC_SEQ, D_MODEL)
    return (y_prompt, y_sample, jnp.stack(states, axis=1))
```

```python
import functools

import numpy as np
import jax
import jax.numpy as jnp
from jax import lax
from jax.experimental import pallas as pl
from jax.experimental.pallas import tpu as pltpu

F32 = jnp.float32
BF16 = jnp.bfloat16

D_MODEL = 2048
BATCH, SEQ = 16, 256
DEC_BATCH, DEC_SEQ = 2, 1024
DEPTH = 2
GRID_W = 64
HEAD_DIM = 128
W_GROUP = D_MODEL // 4
N_HEADS = W_GROUP // HEAD_DIM
POOL_WINDOWS = (2, 4, 8, 16)
POOL_HALO = 8
CHUNK_B = 128
D_FF = 5632
D_IN = 11 * W_GROUP
EPS = 1e-6

N_CTX = BATCH * SEQ
N_LAT = DEC_BATCH * DEC_SEQ
N_TOK = N_CTX + N_LAT
MOD_ROWS = 8

TM = 1024
N_TM = N_TOK // TM
N_TM_CTX = N_CTX // TM
TM_IN = 2048
TN_IN = 512
TF = 512
FFN_CT = 256
FFN_RC = 64
FFN_NOUT = 512
TMIX = 256
N_TMIX_CTX = N_CTX // TMIX
TMIX_PER_LAT = DEC_SEQ // TMIX
TP = 512
N_TP_CTX = N_CTX // TP
TP_PER_LAT = DEC_SEQ // TP
TN_ADA = 2048
NORM_ROWS = 128
CAST_ROWS = 32

HC = 64
HC_LEVELS = 6
HC_FINE_LEVELS = 3
VMEM_LIMIT = 56 * 1024 * 1024


def _silu(x):
    return x * jax.nn.sigmoid(x)


def _dot(a, b):
    return jnp.dot(a, b, preferred_element_type=F32)


def _dot_nt(a, b):
    return lax.dot_general(a, b, (((1,), (1,)), ((), ())), preferred_element_type=F32)


def _params(*sem):
    return pltpu.CompilerParams(dimension_semantics=sem, vmem_limit_bytes=VMEM_LIMIT)


def _adaln_kernel(c_ref, w_ref, b_ref, o_ref):
    s = _silu(c_ref[...]).astype(BF16)
    o_ref[...] = _dot(s, w_ref[...].astype(BF16)) + b_ref[...]


def _adaln(cvec, w_ada, b_ada):
    n_out = w_ada.shape[-1]
    return pl.pallas_call(
        _adaln_kernel,
        grid=(DEPTH, n_out // TN_ADA),
        in_specs=[
            pl.BlockSpec((MOD_ROWS, D_MODEL), lambda l, j: (0, 0)),
            pl.BlockSpec((None, D_MODEL, TN_ADA), lambda l, j: (l, 0, j)),
            pl.BlockSpec((None, 1, TN_ADA), lambda l, j: (l, 0, j)),
        ],
        out_specs=pl.BlockSpec((None, MOD_ROWS, TN_ADA), lambda l, j: (l, 0, j)),
        out_shape=jax.ShapeDtypeStruct((DEPTH, MOD_ROWS, n_out), F32),
        compiler_params=_params("arbitrary", "arbitrary"),
        name="adaln",
    )(cvec, w_ada, b_ada.reshape(DEPTH, 1, n_out))


def _norm_mod_rows(x_ref, h_ref, gain, shift, scale, n_rows):
    gain_eff = gain * (1.0 + scale)

    def body(n, carry):
        r0 = pl.multiple_of(n * NORM_ROWS, NORM_ROWS)
        x = x_ref[pl.ds(r0, NORM_ROWS), :]
        ms = jnp.mean(x * x, axis=-1, keepdims=True)
        h_ref[pl.ds(r0, NORM_ROWS), :] = (x * lax.rsqrt(ms + EPS) * gain_eff + shift).astype(BF16)
        return carry
    lax.fori_loop(0, n_rows // NORM_ROWS, body, 0)


def _mod_row_of_big_tile(i):
    return jnp.maximum(i - (N_TM_CTX - 1), 0)


def _mod_spec(layer, k):
    return pl.BlockSpec((None, MOD_ROWS, D_MODEL), lambda *_: (layer, 0, k))


def _prep_kernel(xp_ref, xs_ref, pos_ref, g_ref, sh_ref, sc_ref, x_ref, h_ref):
    i = pl.program_id(0)
    lat_tile = jnp.maximum(i - N_TP_CTX, 0)
    r = jnp.where(i < N_TP_CTX, 0, 1 + lat_tile // TP_PER_LAT)

    @pl.when(i < N_TP_CTX)
    def _():
        def body(n, carry):
            rows = pl.ds(pl.multiple_of(n * NORM_ROWS, NORM_ROWS), NORM_ROWS)
            x_ref[rows, :] = xp_ref[rows, :]
            return carry
        lax.fori_loop(0, TP // NORM_ROWS, body, 0)

    @pl.when(i >= N_TP_CTX)
    def _():
        def body(n, carry):
            rows = pl.ds(pl.multiple_of(n * NORM_ROWS, NORM_ROWS), NORM_ROWS)
            x_ref[rows, :] = xs_ref[rows, :] + pos_ref[rows, :]
            return carry
        lax.fori_loop(0, TP // NORM_ROWS, body, 0)

    _norm_mod_rows(x_ref, h_ref, g_ref[...], sh_ref[pl.ds(r, 1), :], sc_ref[pl.ds(r, 1), :], TP)


def _prep(x_prompt, x_sample, pos, gain, mod, layer):
    lat = lambda i: jnp.maximum(i - N_TP_CTX, 0)
    return pl.pallas_call(
        _prep_kernel,
        grid=(N_TOK // TP,),
        in_specs=[
            pl.BlockSpec((TP, D_MODEL), lambda i: (jnp.minimum(i, N_TP_CTX - 1), 0)),
            pl.BlockSpec((TP, D_MODEL), lambda i: (lat(i), 0)),
            pl.BlockSpec((TP, D_MODEL), lambda i: (lat(i) % TP_PER_LAT, 0)),
            pl.BlockSpec((1, D_MODEL), lambda i: (0, 0)),
            _mod_spec(layer, 0),
            _mod_spec(layer, 1),
        ],
        out_specs=[pl.BlockSpec((TP, D_MODEL), lambda i: (i, 0)),
                   pl.BlockSpec((TP, D_MODEL), lambda i: (i, 0))],
        out_shape=[jax.ShapeDtypeStruct((N_TOK, D_MODEL), F32),
                   jax.ShapeDtypeStruct((N_TOK, D_MODEL), BF16)],
        compiler_params=_params("arbitrary"),
        name="prep",
    )(x_prompt, x_sample, pos, gain, mod, mod)


def _inproj_kernel(h_ref, w_ref, z_ref):
    z_ref[...] = _dot(h_ref[...], w_ref[...].astype(BF16))


def _inproj(h, w_in, layer):
    return pl.pallas_call(
        _inproj_kernel,
        grid=(N_TOK // TM_IN, D_IN // TN_IN),
        in_specs=[
            pl.BlockSpec((TM_IN, D_MODEL), lambda i, j: (i, 0)),
            pl.BlockSpec((None, D_MODEL, TN_IN), lambda i, j: (layer, 0, j)),
        ],
        out_specs=pl.BlockSpec((TM_IN, TN_IN), lambda i, j: (i, j)),
        out_shape=jax.ShapeDtypeStruct((N_TOK, D_IN), F32),
        compiler_params=_params("arbitrary", "arbitrary"),
        name="inproj",
    )(h, w_in)


def _hgrn_constants():
    t = np.arange(HC)
    run = [(t[None, :] <= t[:, None]), (t[None, :] >= t[:, None])]
    g = np.zeros((2, HC_FINE_LEVELS + 1, HC, HC), np.float32)
    uq = np.ones((2, HC_LEVELS + 1, HC), np.float32)
    bm = np.zeros((HC_LEVELS + 1, HC, HC), np.float32)
    for d in range(2):
        g[d, 0] = run[d]
        for lev in range(HC_LEVELS):
            m = 1 << lev
            base = (t // (2 * m)) * (2 * m)
            if lev < HC_FINE_LEVELS:
                g[d, lev + 1] = run[d][base + m - 1 + d]
            upper = (t % (2 * m)) >= m
            uq[d, lev] = upper if d == 0 else ~upper
    uk = 1.0 - uq
    uk[:, HC_LEVELS] = 1.0
    for lev in range(HC_LEVELS):
        m = 1 << lev
        bm[lev] = (t[:, None] // (2 * m)) == (t[None, :] // (2 * m))
    bm[HC_LEVELS] = np.eye(HC)
    pair = bm[None] * uq[:, :, :, None] * uk[:, :, None, :]
    g = g.reshape(2, (HC_FINE_LEVELS + 1) * HC, HC)
    g3 = np.concatenate([g, g, g], axis=-1)
    lanes = lambda a: np.ascontiguousarray(np.broadcast_to(a[..., None], a.shape + (HEAD_DIM,)))
    return (jnp.asarray(g3, BF16), jnp.asarray(lanes(uq[:, :HC_FINE_LEVELS]), F32),
            jnp.asarray(pair, F32))


def _hgrn_chunk(d, r0, z_ref, lb_ref, g3_ref, uq_ref, pair_ref, s_scr, o_scr):
    rows = pl.ds(r0, HC)
    lb = lb_ref[d:d + 1, :]
    f = lb + (1.0 - lb) * jax.nn.sigmoid(z_ref[rows, (2 + d) * W_GROUP:(3 + d) * W_GROUP])
    kk_all = 1.0 - f
    lf = jnp.log(f)

    l1 = lf.astype(BF16)
    r1 = lf - l1.astype(F32)
    l2 = r1.astype(BF16)
    l3 = (r1 - l2.astype(F32)).astype(BF16)
    sums = _dot(g3_ref[d], jnp.concatenate([l1, l2, l3], axis=0))

    for h in range(N_HEADS):
        cs = slice(h * HEAD_DIM, (h + 1) * HEAD_DIM)
        q = z_ref[rows, cs]
        vb = z_ref[rows, W_GROUP + h * HEAD_DIM:W_GROUP + (h + 1) * HEAD_DIM].astype(BF16)
        kk = kk_all[:, cs]
        cum = sums[0:HC, cs]

        lo, hi = (kk, q) if d == 0 else (q, kk)
        att = _dot_nt(q.astype(BF16), kk.astype(BF16)) * pair_ref[d, HC_LEVELS]
        for lev in range(HC_LEVELS):
            m = 1 << lev
            if lev < HC_FINE_LEVELS:
                bound = sums[(lev + 1) * HC:(lev + 2) * HC, cs]
                rows_qk = jnp.where(uq_ref[d, lev] > 0.5, q, kk)
            else:
                bound = jnp.concatenate(
                    [jnp.broadcast_to(cum[b0 + m - 1 + d:b0 + m + d, :], (2 * m, HEAD_DIM))
                     for b0 in range(0, HC, 2 * m)], axis=0)
                rows_qk = jnp.concatenate(
                    [part for b0 in range(0, HC, 2 * m)
                     for part in (lo[b0:b0 + m], hi[b0 + m:b0 + 2 * m])], axis=0)
            decayed = (rows_qk * jnp.exp(-jnp.abs(cum - bound))).astype(BF16)
            att = att + _dot_nt(decayed, decayed) * pair_ref[d, lev]

        far = cum[HC - 1:HC, :] if d == 0 else cum[0:1, :]
        s_old = s_scr[d * N_HEADS + h]
        q_in = (q * jnp.exp(cum)).astype(BF16)
        o_scr[rows, cs] = _dot(jnp.concatenate([q_in, att.astype(BF16)], axis=1),
                               jnp.concatenate([s_old.astype(BF16), vb], axis=0))

        k_out = kk * jnp.exp(-jnp.abs(cum - far))
        k_ext = jnp.concatenate([k_out, jnp.broadcast_to(jnp.exp(far), (8, HEAD_DIM))], axis=0)
        k_ext_t = k_ext.T
        s_scr[d * N_HEADS + h] = (k_ext_t[:, HC:HC + 1] * s_old
                                  + _dot(k_ext_t[:, 0:HC].astype(BF16), vb))


def _hgrn_kernel(*refs, n_chunks, has_s0, n_cast):
    z_ref, lb_ref, ng_ref, g3_ref, uq_ref, pair_ref = refs[:6]
    k = 6
    s0_ref = st_ref = None
    if has_s0:
        s0_ref = refs[k]
        k += 1
    cast_src = refs[k:k + n_cast]
    k += n_cast
    a_ref = refs[k]
    k += 1
    if not has_s0:
        st_ref = refs[k]
        k += 1
    cast_dst = refs[k:k + n_cast]
    k += n_cast
    s_scr, of_scr, ob_scr = refs[k:k + 3]

    for src, dst in zip(cast_src, cast_dst):
        for r0 in range(0, src.shape[0], CAST_ROWS):
            dst[r0:r0 + CAST_ROWS, :] = src[r0:r0 + CAST_ROWS, :].astype(BF16)

    for d in range(2):
        for h in range(N_HEADS):
            s_scr[d * N_HEADS + h] = (s0_ref[d, h] if has_s0
                                      else jnp.zeros((HEAD_DIM, HEAD_DIM), F32))

    def scan_body(n, carry):
        for d, o_scr in ((0, of_scr), (1, ob_scr)):
            r0 = pl.multiple_of((n if d == 0 else n_chunks - 1 - n) * HC, HC)
            _hgrn_chunk(d, r0, z_ref, lb_ref, g3_ref, uq_ref, pair_ref, s_scr, o_scr)
        return carry
    lax.fori_loop(0, n_chunks, scan_body, 0, unroll=4)

    def out_body(n, carry):
        rows = pl.ds(pl.multiple_of(n * HC, HC), HC)
        o = of_scr[rows, :] + ob_scr[rows, :]
        parts = []
        for h in range(N_HEADS):
            oh = o[:, h * HEAD_DIM:(h + 1) * HEAD_DIM]
            parts.append(oh * lax.rsqrt(jnp.mean(oh * oh, axis=-1, keepdims=True) + EPS))
        gate = z_ref[rows, 4 * W_GROUP:5 * W_GROUP]
        a_ref[rows, :] = jnp.concatenate(parts, axis=-1) * ng_ref[...] * _silu(gate)
        return carry
    lax.fori_loop(0, n_chunks, out_body, 0)

    if st_ref is not None:
        for d in range(2):
            for h in range(N_HEADS):
                st_ref[d, h] = s_scr[d * N_HEADS + h]


def _hgrn(z, lb, norm_g, consts, *, seq_len, n_seq, row_block0, s0=None, layer=0, casts=()):
    has_s0 = s0 is not None
    const_specs = [pl.BlockSpec(c.shape, lambda b, nd=c.ndim: (0,) * nd) for c in consts]
    in_specs = [
        pl.BlockSpec((seq_len, 5 * W_GROUP), lambda b: (row_block0 + b, 0)),
        pl.BlockSpec((2, W_GROUP), lambda b: (0, 0)),
        pl.BlockSpec((1, W_GROUP), lambda b: (0, 0)),
    ] + const_specs
    args = [z, lb, norm_g, *consts]
    if has_s0:
        in_specs.append(pl.BlockSpec((None, None, 2, N_HEADS, HEAD_DIM, HEAD_DIM),
                                     lambda b: (b, layer, 0, 0, 0, 0)))
        args.append(s0)
    out_specs = [pl.BlockSpec((seq_len, W_GROUP), lambda b: (b, 0))]
    out_shape = [jax.ShapeDtypeStruct((n_seq * seq_len, W_GROUP), F32)]
    if not has_s0:
        out_specs.append(pl.BlockSpec((None, 2, N_HEADS, HEAD_DIM, HEAD_DIM),
                                      lambda b: (b, 0, 0, 0, 0)))
        out_shape.append(jax.ShapeDtypeStruct((n_seq, 2, N_HEADS, HEAD_DIM, HEAD_DIM), F32))
    for w in casts:
        slab, cols = w.shape[1] // n_seq, w.shape[2]
        assert slab * n_seq == w.shape[1] and slab % CAST_ROWS == 0
        in_specs.append(pl.BlockSpec((None, slab, cols), lambda b: (layer, b, 0)))
        args.append(w)
        out_specs.append(pl.BlockSpec((slab, cols), lambda b: (b, 0)))
        out_shape.append(jax.ShapeDtypeStruct(w.shape[1:], BF16))
    kern = functools.partial(_hgrn_kernel, n_chunks=seq_len // HC, has_s0=has_s0,
                             n_cast=len(casts))
    return pl.pallas_call(
        kern,
        grid=(n_seq,),
        in_specs=in_specs,
        out_specs=out_specs,
        out_shape=out_shape,
        scratch_shapes=[
            pltpu.VMEM((2 * N_HEADS, HEAD_DIM, HEAD_DIM), F32),
            pltpu.VMEM((seq_len, W_GROUP), F32),
            pltpu.VMEM((seq_len, W_GROUP), F32),
        ],
        compiler_params=_params("arbitrary"),
        name="hgrn_lat" if has_s0 else "hgrn_ctx",
    )(*args)


def _mix_kernel(actx_ref, alat_ref, bu_ref, bv_ref, cb_ref, cc_ref, ch_ref, dx_ref,
                zp_ref, zn_ref,
                x_ref, g1_ref, sgun_ref, wsgu_ref, bsgu_ref, wconv_ref, wpool_ref, pscale_ref,
                wout_ref, o_ref, cat_scr):
    i = pl.program_id(0)
    is_ctx = i < N_TMIX_CTX
    lat_tile = jnp.maximum(i - N_TMIX_CTX, 0)
    j = jnp.where(is_ctx, 0, lat_tile % TMIX_PER_LAT)
    tiles_in_seq = jnp.where(is_ctx, 1, TMIX_PER_LAT)
    keep_prev = jnp.where(j == 0, 0.0, 1.0)
    keep_next = jnp.where(j == tiles_in_seq - 1, 0.0, 1.0)
    mod_row = jnp.where(is_ctx, 0, 1 + lat_tile // TMIX_PER_LAT)
    row = lax.broadcasted_iota(jnp.int32, (TMIX, 1), 0)

    def project(k):
        ks = slice(k * W_GROUP, (k + 1) * W_GROUP)
        return _dot(cat_scr[:, ks], wout_ref[ks, :])

    cat_scr[:, 0:W_GROUP] = jnp.where(is_ctx, actx_ref[...], alat_ref[...]).astype(BF16)
    o_ref[...] = project(0)

    u = jax.nn.gelu(bu_ref[...])
    vv = jax.nn.gelu(bv_ref[...])
    vn = (vv * lax.rsqrt(jnp.mean(vv * vv, axis=-1, keepdims=True) + EPS) * sgun_ref[...]).astype(BF16)
    for n in range(TMIX // CHUNK_B):
        rs = slice(n * CHUNK_B, (n + 1) * CHUNK_B)
        for h in range(N_HEADS):
            cs = slice(h * HEAD_DIM, (h + 1) * HEAD_DIM)
            mixed = _dot(wsgu_ref[h], vn[rs, cs]) + bsgu_ref[:, cs]
            cat_scr[rs, W_GROUP + h * HEAD_DIM:W_GROUP + (h + 1) * HEAD_DIM] = (u[rs, cs] * mixed).astype(BF16)
    o_ref[...] += project(1)

    p = cc_ref[...] * ch_ref[...]
    cc_cols, ch_cols, dx_cols = (slice(k * W_GROUP, (k + 1) * W_GROUP) for k in (8, 9, 10))
    last = slice(POOL_HALO - 1, POOL_HALO)
    p_edge_prev = zp_ref[last, cc_cols] * zp_ref[last, ch_cols] * keep_prev
    p_edge_next = zn_ref[0:1, cc_cols] * zn_ref[0:1, ch_cols] * keep_next
    p_prev = jnp.where(row == 0, p_edge_prev, pltpu.roll(p, 1, 0))
    p_next = jnp.where(row == TMIX - 1, p_edge_next, pltpu.roll(p, TMIX - 1, 0))
    wc = wconv_ref[...]
    c_out = cb_ref[...] * (wc[0:1] * p_prev + wc[1:2] * p + wc[2:3] * p_next)
    cat_scr[:, 2 * W_GROUP:3 * W_GROUP] = c_out.astype(BF16)
    o_ref[...] += project(2)

    xd = dx_ref[...]
    ext = jnp.concatenate([zp_ref[:, dx_cols] * keep_prev, xd, zn_ref[:, dx_cols] * keep_next], axis=0)
    n_ext = TMIX + 2 * POOL_HALO
    back = lambda a, s: pltpu.roll(a, s, 0)
    ahead = lambda a, s: pltpu.roll(a, n_ext - s, 0)
    gd = W_GROUP // len(POOL_WINDOWS)
    sums = [back(ext, 1) + ext]
    for g, s in enumerate((1, 2, 4)):
        wider = sums[-1][:, gd:]
        sums.append(back(wider, s) + ahead(wider, s))
    pos = j * TMIX + row
    seq_len = tiles_in_seq * TMIX
    for g, win in enumerate(POOL_WINDOWS):
        cs = slice(g * gd, (g + 1) * gd)
        cnt = (jnp.minimum(pos + win // 2, seq_len) - jnp.maximum(pos - win // 2, 0)).astype(F32)
        mean = sums[g][POOL_HALO:POOL_HALO + TMIX, 0:gd] / cnt
        dg = _dot((mean - xd[:, cs]).astype(BF16), wpool_ref[g]) * pscale_ref[:, cs]
        cat_scr[:, 3 * W_GROUP + g * gd:3 * W_GROUP + (g + 1) * gd] = dg.astype(BF16)

    y = o_ref[...] + project(3)
    o_ref[...] = x_ref[...] + g1_ref[pl.ds(mod_row, 1), :] * y


def _mix(a_ctx, a_lat, z, x, mod, sgu_norm, w_sgu, b_sgu_full, w_conv, w_pool, pool_scale, w_out,
         layer):
    halo_blocks = TMIX // POOL_HALO
    last_halo = N_TOK // POOL_HALO - 1
    zcol = lambda k: pl.BlockSpec((TMIX, W_GROUP), lambda i: (i, k))
    zprev = pl.BlockSpec((POOL_HALO, D_IN), lambda i: (jnp.maximum(i * halo_blocks - 1, 0), 0))
    znext = pl.BlockSpec((POOL_HALO, D_IN),
                         lambda i: (jnp.minimum((i + 1) * halo_blocks, last_halo), 0))
    full = lambda arr: pl.BlockSpec(arr.shape, lambda i, nd=arr.ndim: (0,) * nd)
    of_layer = lambda arr: pl.BlockSpec((None,) + arr.shape[1:],
                                        lambda i, nd=arr.ndim: (layer,) + (0,) * (nd - 1))
    return pl.pallas_call(
        _mix_kernel,
        grid=(N_TOK // TMIX,),
        in_specs=[
            pl.BlockSpec((TMIX, W_GROUP), lambda i: (jnp.minimum(i, N_TMIX_CTX - 1), 0)),
            pl.BlockSpec((TMIX, W_GROUP), lambda i: (jnp.maximum(i - N_TMIX_CTX, 0), 0)),
            zcol(5), zcol(6), zcol(7), zcol(8), zcol(9), zcol(10),
            zprev, znext,
            pl.BlockSpec((TMIX, D_MODEL), lambda i: (i, 0)),
            _mod_spec(layer, 2),
            full(sgu_norm), of_layer(w_sgu), full(b_sgu_full), full(w_conv), of_layer(w_pool),
            full(pool_scale), full(w_out),
        ],
        out_specs=pl.BlockSpec((TMIX, D_MODEL), lambda i: (i, 0)),
        out_shape=jax.ShapeDtypeStruct((N_TOK, D_MODEL), F32),
        scratch_shapes=[pltpu.VMEM((TMIX, D_MODEL), BF16)],
        compiler_params=_params("arbitrary"),
        name="mix",
    )(a_ctx, a_lat, z, z, z, z, z, z, z, z, x, mod, sgu_norm, w_sgu, b_sgu_full, w_conv, w_pool,
      pool_scale, w_out)


def _ffn_kernel(x_ref, g_ref, sh_ref, sc_ref, g2_ref, wv_ref, wg_ref, cv_ref, cg_ref, wd_ref,
                *rest, emit_next):
    if emit_next:
        gn_ref, shn_ref, scn_ref, o_ref, hn_ref, uv_scr, ug_scr, act_scr = rest
        h_scr = hn_ref
    else:
        o_ref, h_scr, uv_scr, ug_scr, act_scr = rest
    i = pl.program_id(0)
    j = pl.program_id(1)
    r = _mod_row_of_big_tile(i)

    @pl.when(j == 0)
    def _():
        _norm_mod_rows(x_ref, h_scr, g_ref[...], sh_ref[pl.ds(r, 1), :], sc_ref[pl.ds(r, 1), :], TM)

        def zero_body(n, carry):
            rows = pl.ds(pl.multiple_of(n * NORM_ROWS, NORM_ROWS), NORM_ROWS)
            o_ref[rows, :] = jnp.zeros((NORM_ROWS, D_MODEL), F32)
            return carry
        lax.fori_loop(0, TM // NORM_ROWS, zero_body, 0)

    inner_keep = jnp.where(i < N_TM_CTX, 0.0, 1.0)

    def conv3_rows(u_scr, w, r0, cs):
        cur = u_scr[r0:r0 + FFN_RC, cs]
        if r0 == 0:
            prev = jnp.zeros((POOL_HALO, FFN_CT), F32)
        else:
            prev = u_scr[r0 - POOL_HALO:r0, cs]
            if r0 % SEQ == 0:
                prev = prev * inner_keep
        r1 = r0 + FFN_RC
        if r1 == TM:
            nxt = jnp.zeros((POOL_HALO, FFN_CT), F32)
        else:
            nxt = u_scr[r1:r1 + POOL_HALO, cs]
            if r1 % SEQ == 0:
                nxt = nxt * inner_keep
        ext = jnp.concatenate([prev, cur, nxt], axis=0)
        n_ext = FFN_RC + 2 * POOL_HALO
        before = pltpu.roll(ext, 1, 0)[POOL_HALO:POOL_HALO + FFN_RC]
        after = pltpu.roll(ext, n_ext - 1, 0)[POOL_HALO:POOL_HALO + FFN_RC]
        return w[0:1] * before + w[1:2] * cur + w[2:3] * after

    col_tiles = [slice(c * FFN_CT, (c + 1) * FFN_CT) for c in range(TF // FFN_CT)]
    for cs in col_tiles:
        uv_scr[:, cs] = _dot(h_scr[...], wv_ref[:, cs])
        ug_scr[:, cs] = _dot(h_scr[...], wg_ref[:, cs])
    for cs in col_tiles:
        cv = cv_ref[:, cs]
        cg = cg_ref[:, cs]
        for r0 in range(0, TM, FFN_RC):
            val = conv3_rows(uv_scr, cv, r0, cs)
            gate = conv3_rows(ug_scr, cg, r0, cs)
            act_scr[r0:r0 + FFN_RC, cs] = (val * _silu(gate)).astype(BF16)
    for c in range(D_MODEL // FFN_NOUT):
        ns = slice(c * FFN_NOUT, (c + 1) * FFN_NOUT)
        o_ref[:, ns] += _dot(act_scr[...], wd_ref[:, ns])

    @pl.when(j == pl.num_programs(1) - 1)
    def _():
        g2 = g2_ref[pl.ds(r, 1), :]

        def body(n, carry):
            rows = pl.ds(pl.multiple_of(n * NORM_ROWS, NORM_ROWS), NORM_ROWS)
            o_ref[rows, :] = x_ref[rows, :] + g2 * o_ref[rows, :]
            return carry
        lax.fori_loop(0, TM // NORM_ROWS, body, 0)
        if emit_next:
            _norm_mod_rows(o_ref, hn_ref, gn_ref[...], shn_ref[pl.ds(r, 1), :],
                           scn_ref[pl.ds(r, 1), :], TM)


def _ffn(x, gain, mod, w_up, w_conv, w_down, layer, next_gain=None):
    n_f = D_FF // TF
    emit_next = next_gain is not None
    tile = pl.BlockSpec((TM, D_MODEL), lambda i, j: (i, 0))
    next_specs, next_args = [], []
    out_specs = [tile]
    out_shape = [jax.ShapeDtypeStruct((N_TOK, D_MODEL), F32)]
    if emit_next:
        next_specs = [pl.BlockSpec((1, D_MODEL), lambda i, j: (0, 0)),
                      _mod_spec(layer + 1, 0), _mod_spec(layer + 1, 1)]
        next_args = [next_gain, mod, mod]
        out_specs.append(tile)
        out_shape.append(jax.ShapeDtypeStruct((N_TOK, D_MODEL), BF16))
    return pl.pallas_call(
        functools.partial(_ffn_kernel, emit_next=emit_next),
        grid=(N_TM, n_f),
        in_specs=[
            pl.BlockSpec((TM, D_MODEL), lambda i, j: (i, 0)),
            pl.BlockSpec((1, D_MODEL), lambda i, j: (0, 0)),
            _mod_spec(layer, 3),
            _mod_spec(layer, 4),
            _mod_spec(layer, 5),
            pl.BlockSpec((D_MODEL, TF), lambda i, j: (0, j)),
            pl.BlockSpec((D_MODEL, TF), lambda i, j: (0, j + n_f)),
            pl.BlockSpec((None, 3, TF), lambda i, j: (layer, 0, j)),
            pl.BlockSpec((None, 3, TF), lambda i, j: (layer, 0, j + n_f)),
            pl.BlockSpec((TF, D_MODEL), lambda i, j: (j, 0)),
        ] + next_specs,
        out_specs=out_specs,
        out_shape=out_shape,
        scratch_shapes=([] if emit_next else [pltpu.VMEM((TM, D_MODEL), BF16)])
        + [pltpu.VMEM((TM, TF), F32), pltpu.VMEM((TM, TF), F32), pltpu.VMEM((TM, TF), BF16)],
        compiler_params=_params("arbitrary", "arbitrary"),
        name="ffn",
    )(x, gain, mod, mod, mod, w_up, w_up, w_conv, w_conv, w_down, *next_args)


def _final_norm_kernel(x_ref, g_ref, o_ref):
    gain = g_ref[...]

    def body(n, carry):
        rows = pl.ds(pl.multiple_of(n * NORM_ROWS, NORM_ROWS), NORM_ROWS)
        x = x_ref[rows, :]
        o_ref[rows, :] = x * lax.rsqrt(jnp.mean(x * x, axis=-1, keepdims=True) + EPS) * gain
        return carry
    lax.fori_loop(0, TM // NORM_ROWS, body, 0)


def _final_norm(x, gain, tile0, n_tiles):
    return pl.pallas_call(
        _final_norm_kernel,
        grid=(n_tiles,),
        in_specs=[pl.BlockSpec((TM, D_MODEL), lambda i: (tile0 + i, 0)),
                  pl.BlockSpec((1, D_MODEL), lambda i: (0, 0))],
        out_specs=pl.BlockSpec((TM, D_MODEL), lambda i: (i, 0)),
        out_shape=jax.ShapeDtypeStruct((n_tiles * TM, D_MODEL), F32),
        compiler_params=_params("arbitrary"),
        name="final_norm",
    )(x, gain)


def _grid_pos_embed():
    rows = DEC_SEQ // GRID_W
    quarter = D_MODEL // 4
    freq = jnp.exp(-jnp.log(10000.0) * jnp.arange(quarter, dtype=F32) / quarter)[None, :]
    r = jnp.arange(rows, dtype=F32)[:, None] * freq
    col = jnp.arange(GRID_W, dtype=F32)[:, None] * freq
    rep = lambda a: jnp.repeat(a, GRID_W, axis=0)
    til = lambda a: jnp.tile(a, (rows, 1))
    return jnp.concatenate([rep(jnp.sin(r)), rep(jnp.cos(r)), til(jnp.sin(col)), til(jnp.cos(col))], -1)


def kernel(x_prompt, x_sample, c, state_hgrn, c_ctx, w_ada, b_ada, norm_mix, norm_ffn, w_in,
           lb_logits, hgrn_norm, sgu_norm, w_sgu, b_sgu, w_conv_c, w_pool, pool_scale, w_out,
           w_up, w_conv_ffn, w_down, norm_final):
    p = jax.nn.softmax(lb_logits.astype(F32), axis=0)
    lbs = jnp.cumsum(p, axis=0) - p[0:1]

    cvec = jnp.concatenate([c_ctx[None, :], c, jnp.zeros((MOD_ROWS - 1 - DEC_BATCH, D_MODEL), F32)], 0)
    mod = _adaln(cvec, w_ada, b_ada)

    consts = _hgrn_constants()
    w_sgu, w_pool = w_sgu.astype(BF16), w_pool.astype(BF16)
    states = []
    x = None
    for l in range(DEPTH):
        row = lambda a: a[l].reshape(1, -1)
        if l == 0:
            x, h = _prep(x_prompt.reshape(N_CTX, D_MODEL), x_sample.reshape(N_LAT, D_MODEL),
                         _grid_pos_embed(), row(norm_mix), mod, l)
        z = _inproj(h, w_in, l)
        a_ctx, st, w_up_l, w_down_l, w_out_l = _hgrn(
            z, lbs[l], row(hgrn_norm), consts, seq_len=SEQ, n_seq=BATCH, row_block0=0, layer=l,
            casts=(w_up, w_down, w_out))
        a_lat, = _hgrn(z, lbs[l], row(hgrn_norm), consts, seq_len=DEC_SEQ, n_seq=DEC_BATCH,
                       row_block0=N_CTX // DEC_SEQ, s0=state_hgrn, layer=l)
        b_sgu_full = jnp.repeat(b_sgu[l].T, HEAD_DIM, axis=1)
        x = _mix(a_ctx, a_lat, z, x, mod, row(sgu_norm), w_sgu, b_sgu_full, w_conv_c[l], w_pool,
                 row(pool_scale), w_out_l, l)
        if l + 1 < DEPTH:
            x, h = _ffn(x, row(norm_ffn), mod, w_up_l, w_conv_ffn, w_down_l, l,
                        next_gain=norm_mix[l + 1].reshape(1, -1))
        else:
            x, = _ffn(x, row(norm_ffn), mod, w_up_l, w_conv_ffn, w_down_l, l)
        states.append(st)

    gain = norm_final.reshape(1, -1)
    y_prompt = _final_norm(x, gain, 0, N_TM_CTX).reshape(BATCH, SEQ, D_MODEL)
    y_sample = _final_norm(x, gain, N_TM_CTX, N_TM - N_TM_CTX).reshape(DEC_BATCH, DEC_SEQ, D_MODEL)
    return (y_prompt, y_sample, jnp.stack(states, axis=1))
```

```python
import functools

import numpy as np
import jax
import jax.numpy as jnp
from jax import lax
from jax.experimental import pallas as pl
from jax.experimental.pallas import tpu as pltpu

F32 = jnp.float32
BF16 = jnp.bfloat16

D_MODEL = 2048
BATCH, SEQ = 16, 256
DEC_BATCH, DEC_SEQ = 2, 1024
DEPTH = 2
GRID_W = 64
HEAD_DIM = 128
W_GROUP = D_MODEL // 4
N_HEADS = W_GROUP // HEAD_DIM
POOL_WINDOWS = (2, 4, 8, 16)
POOL_HALO = 8
CHUNK_B = 128
D_FF = 5632
D_IN = 11 * W_GROUP
EPS = 1e-6

N_CTX = BATCH * SEQ
N_LAT = DEC_BATCH * DEC_SEQ
N_TOK = N_CTX + N_LAT
MOD_ROWS = 8

TM = 1024
N_TM = N_TOK // TM
N_TM_CTX = N_CTX // TM
TM_IN = 2048
TN_IN = 512
TF = 512
FFN_CT = 256
FFN_RC = 64
FFN_NOUT = 512
TMIX = 256
N_TMIX_CTX = N_CTX // TMIX
TMIX_PER_LAT = DEC_SEQ // TMIX
TP = 512
N_TP_CTX = N_CTX // TP
TP_PER_LAT = DEC_SEQ // TP
TN_ADA = 1024
NORM_ROWS = 128
CAST_ROWS = 32

HC = 64
HC_LEVELS = 6
HC_FINE_LEVELS = 3
VMEM_LIMIT = 56 * 1024 * 1024


def _silu(x):
    return x * jax.nn.sigmoid(x)


def _dot(a, b):
    return jnp.dot(a, b, preferred_element_type=F32)


def _dot_nt(a, b):
    return lax.dot_general(a, b, (((1,), (1,)), ((), ())), preferred_element_type=F32)


def _params(*sem):
    return pltpu.CompilerParams(dimension_semantics=sem, vmem_limit_bytes=VMEM_LIMIT)


def _adaln_kernel(c_ref, w_ref, b_ref, o_ref):
    s = _silu(c_ref[...]).astype(BF16)
    o_ref[...] = _dot(s, w_ref[...].astype(BF16)) + b_ref[...]


def _adaln(cvec, w_ada, b_ada):
    n_out = w_ada.shape[-1]
    return pl.pallas_call(
        _adaln_kernel,
        grid=(DEPTH, n_out // TN_ADA),
        in_specs=[
            pl.BlockSpec((MOD_ROWS, D_MODEL), lambda l, j: (0, 0)),
            pl.BlockSpec((None, D_MODEL, TN_ADA), lambda l, j: (l, 0, j)),
            pl.BlockSpec((None, 1, TN_ADA), lambda l, j: (l, 0, j)),
        ],
        out_specs=pl.BlockSpec((None, MOD_ROWS, TN_ADA), lambda l, j: (l, 0, j)),
        out_shape=jax.ShapeDtypeStruct((DEPTH, MOD_ROWS, n_out), F32),
        compiler_params=_params("arbitrary", "arbitrary"),
        name="adaln",
    )(cvec, w_ada, b_ada.reshape(DEPTH, 1, n_out))


def _norm_mod_rows(x_ref, h_ref, gain, shift, scale, n_rows):
    gain_eff = gain * (1.0 + scale)

    def body(n, carry):
        r0 = pl.multiple_of(n * NORM_ROWS, NORM_ROWS)
        x = x_ref[pl.ds(r0, NORM_ROWS), :]
        ms = jnp.mean(x * x, axis=-1, keepdims=True)
        h_ref[pl.ds(r0, NORM_ROWS), :] = (x * lax.rsqrt(ms + EPS) * gain_eff + shift).astype(BF16)
        return carry
    lax.fori_loop(0, n_rows // NORM_ROWS, body, 0)


def _mod_row_of_big_tile(i):
    return jnp.maximum(i - (N_TM_CTX - 1), 0)


def _mod_spec(layer, k):
    return pl.BlockSpec((None, MOD_ROWS, D_MODEL), lambda *_: (layer, 0, k))


def _prep_kernel(xp_ref, xs_ref, pos_ref, g_ref, sh_ref, sc_ref, x_ref, h_ref):
    i = pl.program_id(0)
    lat_tile = jnp.maximum(i - N_TP_CTX, 0)
    r = jnp.where(i < N_TP_CTX, 0, 1 + lat_tile // TP_PER_LAT)

    @pl.when(i < N_TP_CTX)
    def _():
        def body(n, carry):
            rows = pl.ds(pl.multiple_of(n * NORM_ROWS, NORM_ROWS), NORM_ROWS)
            x_ref[rows, :] = xp_ref[rows, :]
            return carry
        lax.fori_loop(0, TP // NORM_ROWS, body, 0)

    @pl.when(i >= N_TP_CTX)
    def _():
        def body(n, carry):
            rows = pl.ds(pl.multiple_of(n * NORM_ROWS, NORM_ROWS), NORM_ROWS)
            x_ref[rows, :] = xs_ref[rows, :] + pos_ref[rows, :]
            return carry
        lax.fori_loop(0, TP // NORM_ROWS, body, 0)

    _norm_mod_rows(x_ref, h_ref, g_ref[...], sh_ref[pl.ds(r, 1), :], sc_ref[pl.ds(r, 1), :], TP)


def _prep(x_prompt, x_sample, pos, gain, mod, layer):
    lat = lambda i: jnp.maximum(i - N_TP_CTX, 0)
    return pl.pallas_call(
        _prep_kernel,
        grid=(N_TOK // TP,),
        in_specs=[
            pl.BlockSpec((TP, D_MODEL), lambda i: (jnp.minimum(i, N_TP_CTX - 1), 0)),
            pl.BlockSpec((TP, D_MODEL), lambda i: (lat(i), 0)),
            pl.BlockSpec((TP, D_MODEL), lambda i: (lat(i) % TP_PER_LAT, 0)),
            pl.BlockSpec((1, D_MODEL), lambda i: (0, 0)),
            _mod_spec(layer, 0),
            _mod_spec(layer, 1),
        ],
        out_specs=[pl.BlockSpec((TP, D_MODEL), lambda i: (i, 0)),
                   pl.BlockSpec((TP, D_MODEL), lambda i: (i, 0))],
        out_shape=[jax.ShapeDtypeStruct((N_TOK, D_MODEL), F32),
                   jax.ShapeDtypeStruct((N_TOK, D_MODEL), BF16)],
        compiler_params=_params("arbitrary"),
        name="prep",
    )(x_prompt, x_sample, pos, gain, mod, mod)


def _inproj_kernel(h_ref, w_ref, z_ref):
    z_ref[...] = _dot(h_ref[...], w_ref[...].astype(BF16))


def _inproj(h, w_in, layer):
    return pl.pallas_call(
        _inproj_kernel,
        grid=(N_TOK // TM_IN, D_IN // TN_IN),
        in_specs=[
            pl.BlockSpec((TM_IN, D_MODEL), lambda i, j: (i, 0)),
            pl.BlockSpec((None, D_MODEL, TN_IN), lambda i, j: (layer, 0, j)),
        ],
        out_specs=pl.BlockSpec((TM_IN, TN_IN), lambda i, j: (i, j)),
        out_shape=jax.ShapeDtypeStruct((N_TOK, D_IN), F32),
        compiler_params=_params("arbitrary", "arbitrary"),
        name="inproj",
    )(h, w_in)


def _hgrn_constants():
    t = np.arange(HC)
    run = [(t[None, :] <= t[:, None]), (t[None, :] >= t[:, None])]
    g = np.zeros((2, HC_FINE_LEVELS + 1, HC, HC), np.float32)
    uq = np.ones((2, HC_LEVELS + 1, HC), np.float32)
    bm = np.zeros((HC_LEVELS + 1, HC, HC), np.float32)
    for d in range(2):
        g[d, 0] = run[d]
        for lev in range(HC_LEVELS):
            m = 1 << lev
            base = (t // (2 * m)) * (2 * m)
            if lev < HC_FINE_LEVELS:
                g[d, lev + 1] = run[d][base + m - 1 + d]
            upper = (t % (2 * m)) >= m
            uq[d, lev] = upper if d == 0 else ~upper
    uk = 1.0 - uq
    uk[:, HC_LEVELS] = 1.0
    for lev in range(HC_LEVELS):
        m = 1 << lev
        bm[lev] = (t[:, None] // (2 * m)) == (t[None, :] // (2 * m))
    bm[HC_LEVELS] = np.eye(HC)
    pair = bm[None] * uq[:, :, :, None] * uk[:, :, None, :]
    g = g.reshape(2, (HC_FINE_LEVELS + 1) * HC, HC)
    g3 = np.concatenate([g, g, g], axis=-1)
    lanes = lambda a: np.ascontiguousarray(np.broadcast_to(a[..., None], a.shape + (HEAD_DIM,)))
    return (jnp.asarray(g3, BF16), jnp.asarray(lanes(uq[:, :HC_FINE_LEVELS]), F32),
            jnp.asarray(pair, F32))


def _hgrn_chunk(d, r0, z_ref, lb_ref, g3_ref, uq_ref, pair_ref, s_scr, o_scr):
    rows = pl.ds(r0, HC)
    lb = lb_ref[d:d + 1, :]
    f = lb + (1.0 - lb) * jax.nn.sigmoid(z_ref[rows, (2 + d) * W_GROUP:(3 + d) * W_GROUP])
    kk_all = 1.0 - f
    lf = jnp.log(f)

    l1 = lf.astype(BF16)
    r1 = lf - l1.astype(F32)
    l2 = r1.astype(BF16)
    l3 = (r1 - l2.astype(F32)).astype(BF16)
    sums = _dot(g3_ref[d], jnp.concatenate([l1, l2, l3], axis=0))

    for h in range(N_HEADS):
        cs = slice(h * HEAD_DIM, (h + 1) * HEAD_DIM)
        q = z_ref[rows, cs]
        vb = z_ref[rows, W_GROUP + h * HEAD_DIM:W_GROUP + (h + 1) * HEAD_DIM].astype(BF16)
        kk = kk_all[:, cs]
        cum = sums[0:HC, cs]

        lo, hi = (kk, q) if d == 0 else (q, kk)
        att = _dot_nt(q.astype(BF16), kk.astype(BF16)) * pair_ref[d, HC_LEVELS]
        for lev in range(HC_LEVELS):
            m = 1 << lev
            if lev < HC_FINE_LEVELS:
                bound = sums[(lev + 1) * HC:(lev + 2) * HC, cs]
                rows_qk = jnp.where(uq_ref[d, lev] > 0.5, q, kk)
            else:
                bound = jnp.concatenate(
                    [jnp.broadcast_to(cum[b0 + m - 1 + d:b0 + m + d, :], (2 * m, HEAD_DIM))
                     for b0 in range(0, HC, 2 * m)], axis=0)
                rows_qk = jnp.concatenate(
                    [part for b0 in range(0, HC, 2 * m)
                     for part in (lo[b0:b0 + m], hi[b0 + m:b0 + 2 * m])], axis=0)
            decayed = (rows_qk * jnp.exp(-jnp.abs(cum - bound))).astype(BF16)
            att = att + _dot_nt(decayed, decayed) * pair_ref[d, lev]

        far = cum[HC - 1:HC, :] if d == 0 else cum[0:1, :]
        s_old = s_scr[d * N_HEADS + h]
        q_in = (q * jnp.exp(cum)).astype(BF16)
        o_scr[rows, cs] = _dot(jnp.concatenate([q_in, att.astype(BF16)], axis=1),
                               jnp.concatenate([s_old.astype(BF16), vb], axis=0))

        k_out = kk * jnp.exp(-jnp.abs(cum - far))
        k_ext = jnp.concatenate([k_out, jnp.broadcast_to(jnp.exp(far), (8, HEAD_DIM))], axis=0)
        k_ext_t = k_ext.T
        s_scr[d * N_HEADS + h] = (k_ext_t[:, HC:HC + 1] * s_old
                                  + _dot(k_ext_t[:, 0:HC].astype(BF16), vb))


def _hgrn_kernel(*refs, n_chunks, has_s0, n_cast):
    z_ref, lb_ref, ng_ref, g3_ref, uq_ref, pair_ref = refs[:6]
    k = 6
    s0_ref = st_ref = None
    if has_s0:
        s0_ref = refs[k]
        k += 1
    cast_src = refs[k:k + n_cast]
    k += n_cast
    a_ref = refs[k]
    k += 1
    if not has_s0:
        st_ref = refs[k]
        k += 1
    cast_dst = refs[k:k + n_cast]
    k += n_cast
    s_scr, of_scr, ob_scr = refs[k:k + 3]

    for src, dst in zip(cast_src, cast_dst):
        for r0 in range(0, src.shape[0], CAST_ROWS):
            dst[r0:r0 + CAST_ROWS, :] = src[r0:r0 + CAST_ROWS, :].astype(BF16)

    for d in range(2):
        for h in range(N_HEADS):
            s_scr[d * N_HEADS + h] = (s0_ref[d, h] if has_s0
                                      else jnp.zeros((HEAD_DIM, HEAD_DIM), F32))

    def scan_body(n, carry):
        for d, o_scr in ((0, of_scr), (1, ob_scr)):
            r0 = pl.multiple_of((n if d == 0 else n_chunks - 1 - n) * HC, HC)
            _hgrn_chunk(d, r0, z_ref, lb_ref, g3_ref, uq_ref, pair_ref, s_scr, o_scr)
        return carry
    lax.fori_loop(0, n_chunks, scan_body, 0, unroll=4)

    def out_body(n, carry):
        rows = pl.ds(pl.multiple_of(n * HC, HC), HC)
        o = of_scr[rows, :] + ob_scr[rows, :]
        parts = []
        for h in range(N_HEADS):
            oh = o[:, h * HEAD_DIM:(h + 1) * HEAD_DIM]
            parts.append(oh * lax.rsqrt(jnp.mean(oh * oh, axis=-1, keepdims=True) + EPS))
        gate = z_ref[rows, 4 * W_GROUP:5 * W_GROUP]
        a_ref[rows, :] = jnp.concatenate(parts, axis=-1) * ng_ref[...] * _silu(gate)
        return carry
    lax.fori_loop(0, n_chunks, out_body, 0)

    if st_ref is not None:
        for d in range(2):
            for h in range(N_HEADS):
                st_ref[d, h] = s_scr[d * N_HEADS + h]


def _hgrn(z, lb, norm_g, consts, *, seq_len, n_seq, row_block0, s0=None, layer=0, casts=()):
    has_s0 = s0 is not None
    const_specs = [pl.BlockSpec(c.shape, lambda b, nd=c.ndim: (0,) * nd) for c in consts]
    in_specs = [
        pl.BlockSpec((seq_len, 5 * W_GROUP), lambda b: (row_block0 + b, 0)),
        pl.BlockSpec((2, W_GROUP), lambda b: (0, 0)),
        pl.BlockSpec((1, W_GROUP), lambda b: (0, 0)),
    ] + const_specs
    args = [z, lb, norm_g, *consts]
    if has_s0:
        in_specs.append(pl.BlockSpec((None, None, 2, N_HEADS, HEAD_DIM, HEAD_DIM),
                                     lambda b: (b, layer, 0, 0, 0, 0)))
        args.append(s0)
    out_specs = [pl.BlockSpec((seq_len, W_GROUP), lambda b: (b, 0))]
    out_shape = [jax.ShapeDtypeStruct((n_seq * seq_len, W_GROUP), F32)]
    if not has_s0:
        out_specs.append(pl.BlockSpec((None, 2, N_HEADS, HEAD_DIM, HEAD_DIM),
                                      lambda b: (b, 0, 0, 0, 0)))
        out_shape.append(jax.ShapeDtypeStruct((n_seq, 2, N_HEADS, HEAD_DIM, HEAD_DIM), F32))
    for w in casts:
        slab, cols = w.shape[1] // n_seq, w.shape[2]
        assert slab * n_seq == w.shape[1] and slab % CAST_ROWS == 0
        in_specs.append(pl.BlockSpec((None, slab, cols), lambda b: (layer, b, 0)))
        args.append(w)
        out_specs.append(pl.BlockSpec((slab, cols), lambda b: (b, 0)))
        out_shape.append(jax.ShapeDtypeStruct(w.shape[1:], BF16))
    kern = functools.partial(_hgrn_kernel, n_chunks=seq_len // HC, has_s0=has_s0,
                             n_cast=len(casts))
    return pl.pallas_call(
        kern,
        grid=(n_seq,),
        in_specs=in_specs,
        out_specs=out_specs,
        out_shape=out_shape,
        scratch_shapes=[
            pltpu.VMEM((2 * N_HEADS, HEAD_DIM, HEAD_DIM), F32),
            pltpu.VMEM((seq_len, W_GROUP), F32),
            pltpu.VMEM((seq_len, W_GROUP), F32),
        ],
        compiler_params=_params("arbitrary"),
        name="hgrn_lat" if has_s0 else "hgrn_ctx",
    )(*args)


def _mix_kernel(actx_ref, alat_ref, bu_ref, bv_ref, cb_ref, cc_ref, ch_ref, dx_ref,
                zp_ref, zn_ref,
                x_ref, g1_ref, sgun_ref, wsgu_ref, bsgu_ref, wconv_ref, wpool_ref, pscale_ref,
                wout_ref, o_ref, cat_scr):
    i = pl.program_id(0)
    is_ctx = i < N_TMIX_CTX
    lat_tile = jnp.maximum(i - N_TMIX_CTX, 0)
    j = jnp.where(is_ctx, 0, lat_tile % TMIX_PER_LAT)
    tiles_in_seq = jnp.where(is_ctx, 1, TMIX_PER_LAT)
    keep_prev = jnp.where(j == 0, 0.0, 1.0)
    keep_next = jnp.where(j == tiles_in_seq - 1, 0.0, 1.0)
    mod_row = jnp.where(is_ctx, 0, 1 + lat_tile // TMIX_PER_LAT)
    row = lax.broadcasted_iota(jnp.int32, (TMIX, 1), 0)

    def project(k):
        ks = slice(k * W_GROUP, (k + 1) * W_GROUP)
        return _dot(cat_scr[:, ks], wout_ref[ks, :])

    cat_scr[:, 0:W_GROUP] = jnp.where(is_ctx, actx_ref[...], alat_ref[...]).astype(BF16)
    o_ref[...] = project(0)

    u = jax.nn.gelu(bu_ref[...])
    vv = jax.nn.gelu(bv_ref[...])
    vn = (vv * lax.rsqrt(jnp.mean(vv * vv, axis=-1, keepdims=True) + EPS) * sgun_ref[...]).astype(BF16)
    for n in range(TMIX // CHUNK_B):
        rs = slice(n * CHUNK_B, (n + 1) * CHUNK_B)
        for h in range(N_HEADS):
            cs = slice(h * HEAD_DIM, (h + 1) * HEAD_DIM)
            mixed = _dot(wsgu_ref[h], vn[rs, cs]) + bsgu_ref[:, cs]
            cat_scr[rs, W_GROUP + h * HEAD_DIM:W_GROUP + (h + 1) * HEAD_DIM] = (u[rs, cs] * mixed).astype(BF16)
    o_ref[...] += project(1)

    p = cc_ref[...] * ch_ref[...]
    cc_cols, ch_cols, dx_cols = (slice(k * W_GROUP, (k + 1) * W_GROUP) for k in (8, 9, 10))
    last = slice(POOL_HALO - 1, POOL_HALO)
    p_edge_prev = zp_ref[last, cc_cols] * zp_ref[last, ch_cols] * keep_prev
    p_edge_next = zn_ref[0:1, cc_cols] * zn_ref[0:1, ch_cols] * keep_next
    p_prev = jnp.where(row == 0, p_edge_prev, pltpu.roll(p, 1, 0))
    p_next = jnp.where(row == TMIX - 1, p_edge_next, pltpu.roll(p, TMIX - 1, 0))
    wc = wconv_ref[...]
    c_out = cb_ref[...] * (wc[0:1] * p_prev + wc[1:2] * p + wc[2:3] * p_next)
    cat_scr[:, 2 * W_GROUP:3 * W_GROUP] = c_out.astype(BF16)
    o_ref[...] += project(2)

    xd = dx_ref[...]
    ext = jnp.concatenate([zp_ref[:, dx_cols] * keep_prev, xd, zn_ref[:, dx_cols] * keep_next], axis=0)
    n_ext = TMIX + 2 * POOL_HALO
    back = lambda a, s: pltpu.roll(a, s, 0)
    ahead = lambda a, s: pltpu.roll(a, n_ext - s, 0)
    gd = W_GROUP // len(POOL_WINDOWS)
    sums = [back(ext, 1) + ext]
    for g, s in enumerate((1, 2, 4)):
        wider = sums[-1][:, gd:]
        sums.append(back(wider, s) + ahead(wider, s))
    pos = j * TMIX + row
    seq_len = tiles_in_seq * TMIX
    for g, win in enumerate(POOL_WINDOWS):
        cs = slice(g * gd, (g + 1) * gd)
        cnt = (jnp.minimum(pos + win // 2, seq_len) - jnp.maximum(pos - win // 2, 0)).astype(F32)
        mean = sums[g][POOL_HALO:POOL_HALO + TMIX, 0:gd] / cnt
        dg = _dot((mean - xd[:, cs]).astype(BF16), wpool_ref[g]) * pscale_ref[:, cs]
        cat_scr[:, 3 * W_GROUP + g * gd:3 * W_GROUP + (g + 1) * gd] = dg.astype(BF16)

    y = o_ref[...] + project(3)
    o_ref[...] = x_ref[...] + g1_ref[pl.ds(mod_row, 1), :] * y


def _mix(a_ctx, a_lat, z, x, mod, sgu_norm, w_sgu, b_sgu_full, w_conv, w_pool, pool_scale, w_out,
         layer):
    halo_blocks = TMIX // POOL_HALO
    last_halo = N_TOK // POOL_HALO - 1
    zcol = lambda k: pl.BlockSpec((TMIX, W_GROUP), lambda i: (i, k))
    zprev = pl.BlockSpec((POOL_HALO, D_IN), lambda i: (jnp.maximum(i * halo_blocks - 1, 0), 0))
    znext = pl.BlockSpec((POOL_HALO, D_IN),
                         lambda i: (jnp.minimum((i + 1) * halo_blocks, last_halo), 0))
    full = lambda arr: pl.BlockSpec(arr.shape, lambda i, nd=arr.ndim: (0,) * nd)
    of_layer = lambda arr: pl.BlockSpec((None,) + arr.shape[1:],
                                        lambda i, nd=arr.ndim: (layer,) + (0,) * (nd - 1))
    return pl.pallas_call(
        _mix_kernel,
        grid=(N_TOK // TMIX,),
        in_specs=[
            pl.BlockSpec((TMIX, W_GROUP), lambda i: (jnp.minimum(i, N_TMIX_CTX - 1), 0)),
            pl.BlockSpec((TMIX, W_GROUP), lambda i: (jnp.maximum(i - N_TMIX_CTX, 0), 0)),
            zcol(5), zcol(6), zcol(7), zcol(8), zcol(9), zcol(10),
            zprev, znext,
            pl.BlockSpec((TMIX, D_MODEL), lambda i: (i, 0)),
            _mod_spec(layer, 2),
            full(sgu_norm), of_layer(w_sgu), full(b_sgu_full), full(w_conv), of_layer(w_pool),
            full(pool_scale), full(w_out),
        ],
        out_specs=pl.BlockSpec((TMIX, D_MODEL), lambda i: (i, 0)),
        out_shape=jax.ShapeDtypeStruct((N_TOK, D_MODEL), F32),
        scratch_shapes=[pltpu.VMEM((TMIX, D_MODEL), BF16)],
        compiler_params=_params("arbitrary"),
        name="mix",
    )(a_ctx, a_lat, z, z, z, z, z, z, z, z, x, mod, sgu_norm, w_sgu, b_sgu_full, w_conv, w_pool,
      pool_scale, w_out)


def _ffn_kernel(x_ref, g_ref, sh_ref, sc_ref, g2_ref, wv_ref, wg_ref, cv_ref, cg_ref, wd_ref,
                *rest, emit_next):
    if emit_next:
        gn_ref, shn_ref, scn_ref, o_ref, hn_ref, uv_scr, ug_scr, act_scr = rest
        h_scr = hn_ref
    else:
        o_ref, h_scr, uv_scr, ug_scr, act_scr = rest
    i = pl.program_id(0)
    j = pl.program_id(1)
    r = _mod_row_of_big_tile(i)

    @pl.when(j == 0)
    def _():
        _norm_mod_rows(x_ref, h_scr, g_ref[...], sh_ref[pl.ds(r, 1), :], sc_ref[pl.ds(r, 1), :], TM)

        def zero_body(n, carry):
            rows = pl.ds(pl.multiple_of(n * NORM_ROWS, NORM_ROWS), NORM_ROWS)
            o_ref[rows, :] = jnp.zeros((NORM_ROWS, D_MODEL), F32)
            return carry
        lax.fori_loop(0, TM // NORM_ROWS, zero_body, 0)

    inner_keep = jnp.where(i < N_TM_CTX, 0.0, 1.0)

    def conv3_rows(u_scr, w, r0, cs):
        cur = u_scr[r0:r0 + FFN_RC, cs]
        if r0 == 0:
            prev = jnp.zeros((POOL_HALO, FFN_CT), F32)
        else:
            prev = u_scr[r0 - POOL_HALO:r0, cs]
            if r0 % SEQ == 0:
                prev = prev * inner_keep
        r1 = r0 + FFN_RC
        if r1 == TM:
            nxt = jnp.zeros((POOL_HALO, FFN_CT), F32)
        else:
            nxt = u_scr[r1:r1 + POOL_HALO, cs]
            if r1 % SEQ == 0:
                nxt = nxt * inner_keep
        ext = jnp.concatenate([prev, cur, nxt], axis=0)
        n_ext = FFN_RC + 2 * POOL_HALO
        before = pltpu.roll(ext, 1, 0)[POOL_HALO:POOL_HALO + FFN_RC]
        after = pltpu.roll(ext, n_ext - 1, 0)[POOL_HALO:POOL_HALO + FFN_RC]
        return w[0:1] * before + w[1:2] * cur + w[2:3] * after

    col_tiles = [slice(c * FFN_CT, (c + 1) * FFN_CT) for c in range(TF // FFN_CT)]
    for cs in col_tiles:
        uv_scr[:, cs] = _dot(h_scr[...], wv_ref[:, cs])
        ug_scr[:, cs] = _dot(h_scr[...], wg_ref[:, cs])
    for cs in col_tiles:
        cv = cv_ref[:, cs]
        cg = cg_ref[:, cs]
        for r0 in range(0, TM, FFN_RC):
            val = conv3_rows(uv_scr, cv, r0, cs)
            gate = conv3_rows(ug_scr, cg, r0, cs)
            act_scr[r0:r0 + FFN_RC, cs] = (val * _silu(gate)).astype(BF16)
    for c in range(D_MODEL // FFN_NOUT):
        ns = slice(c * FFN_NOUT, (c + 1) * FFN_NOUT)
        o_ref[:, ns] += _dot(act_scr[...], wd_ref[:, ns])

    @pl.when(j == pl.num_programs(1) - 1)
    def _():
        g2 = g2_ref[pl.ds(r, 1), :]

        def body(n, carry):
            rows = pl.ds(pl.multiple_of(n * NORM_ROWS, NORM_ROWS), NORM_ROWS)
            o_ref[rows, :] = x_ref[rows, :] + g2 * o_ref[rows, :]
            return carry
        lax.fori_loop(0, TM // NORM_ROWS, body, 0)
        if emit_next:
            _norm_mod_rows(o_ref, hn_ref, gn_ref[...], shn_ref[pl.ds(r, 1), :],
                           scn_ref[pl.ds(r, 1), :], TM)


def _ffn(x, gain, mod, w_up, w_conv, w_down, layer, next_gain=None):
    n_f = D_FF // TF
    emit_next = next_gain is not None
    tile = pl.BlockSpec((TM, D_MODEL), lambda i, j: (i, 0))
    next_specs, next_args = [], []
    out_specs = [tile]
    out_shape = [jax.ShapeDtypeStruct((N_TOK, D_MODEL), F32)]
    if emit_next:
        next_specs = [pl.BlockSpec((1, D_MODEL), lambda i, j: (0, 0)),
                      _mod_spec(layer + 1, 0), _mod_spec(layer + 1, 1)]
        next_args = [next_gain, mod, mod]
        out_specs.append(tile)
        out_shape.append(jax.ShapeDtypeStruct((N_TOK, D_MODEL), BF16))
    return pl.pallas_call(
        functools.partial(_ffn_kernel, emit_next=emit_next),
        grid=(N_TM, n_f),
        in_specs=[
            pl.BlockSpec((TM, D_MODEL), lambda i, j: (i, 0)),
            pl.BlockSpec((1, D_MODEL), lambda i, j: (0, 0)),
            _mod_spec(layer, 3),
            _mod_spec(layer, 4),
            _mod_spec(layer, 5),
            pl.BlockSpec((D_MODEL, TF), lambda i, j: (0, j)),
            pl.BlockSpec((D_MODEL, TF), lambda i, j: (0, j + n_f)),
            pl.BlockSpec((None, 3, TF), lambda i, j: (layer, 0, j)),
            pl.BlockSpec((None, 3, TF), lambda i, j: (layer, 0, j + n_f)),
            pl.BlockSpec((TF, D_MODEL), lambda i, j: (j, 0)),
        ] + next_specs,
        out_specs=out_specs,
        out_shape=out_shape,
        scratch_shapes=([] if emit_next else [pltpu.VMEM((TM, D_MODEL), BF16)])
        + [pltpu.VMEM((TM, TF), F32), pltpu.VMEM((TM, TF), F32), pltpu.VMEM((TM, TF), BF16)],
        compiler_params=_params("arbitrary", "arbitrary"),
        name="ffn",
    )(x, gain, mod, mod, mod, w_up, w_up, w_conv, w_conv, w_down, *next_args)


def _final_norm_kernel(x_ref, g_ref, o_ref):
    gain = g_ref[...]

    def body(n, carry):
        rows = pl.ds(pl.multiple_of(n * NORM_ROWS, NORM_ROWS), NORM_ROWS)
        x = x_ref[rows, :]
        o_ref[rows, :] = x * lax.rsqrt(jnp.mean(x * x, axis=-1, keepdims=True) + EPS) * gain
        return carry
    lax.fori_loop(0, TM // NORM_ROWS, body, 0)


def _final_norm(x, gain, tile0, n_tiles):
    return pl.pallas_call(
        _final_norm_kernel,
        grid=(n_tiles,),
        in_specs=[pl.BlockSpec((TM, D_MODEL), lambda i: (tile0 + i, 0)),
                  pl.BlockSpec((1, D_MODEL), lambda i: (0, 0))],
        out_specs=pl.BlockSpec((TM, D_MODEL), lambda i: (i, 0)),
        out_shape=jax.ShapeDtypeStruct((n_tiles * TM, D_MODEL), F32),
        compiler_params=_params("arbitrary"),
        name="final_norm",
    )(x, gain)


def _grid_pos_embed():
    rows = DEC_SEQ // GRID_W
    quarter = D_MODEL // 4
    freq = jnp.exp(-jnp.log(10000.0) * jnp.arange(quarter, dtype=F32) / quarter)[None, :]
    r = jnp.arange(rows, dtype=F32)[:, None] * freq
    col = jnp.arange(GRID_W, dtype=F32)[:, None] * freq
    rep = lambda a: jnp.repeat(a, GRID_W, axis=0)
    til = lambda a: jnp.tile(a, (rows, 1))
    return jnp.concatenate([rep(jnp.sin(r)), rep(jnp.cos(r)), til(jnp.sin(col)), til(jnp.cos(col))], -1)


def kernel(x_prompt, x_sample, c, state_hgrn, c_ctx, w_ada, b_ada, norm_mix, norm_ffn, w_in,
           lb_logits, hgrn_norm, sgu_norm, w_sgu, b_sgu, w_conv_c, w_pool, pool_scale, w_out,
           w_up, w_conv_ffn, w_down, norm_final):
    p = jax.nn.softmax(lb_logits.astype(F32), axis=0)
    lbs = jnp.cumsum(p, axis=0) - p[0:1]

    cvec = jnp.concatenate([c_ctx[None, :], c, jnp.zeros((MOD_ROWS - 1 - DEC_BATCH, D_MODEL), F32)], 0)
    mod = _adaln(cvec, w_ada, b_ada)

    consts = _hgrn_constants()
    w_sgu, w_pool = w_sgu.astype(BF16), w_pool.astype(BF16)
    states = []
    x = None
    for l in range(DEPTH):
        row = lambda a: a[l].reshape(1, -1)
        if l == 0:
            x, h = _prep(x_prompt.reshape(N_CTX, D_MODEL), x_sample.reshape(N_LAT, D_MODEL),
                         _grid_pos_embed(), row(norm_mix), mod, l)
        z = _inproj(h, w_in, l)
        a_ctx, st, w_up_l, w_down_l, w_out_l = _hgrn(
            z, lbs[l], row(hgrn_norm), consts, seq_len=SEQ, n_seq=BATCH, row_block0=0, layer=l,
            casts=(w_up, w_down, w_out))
        a_lat, = _hgrn(z, lbs[l], row(hgrn_norm), consts, seq_len=DEC_SEQ, n_seq=DEC_BATCH,
                       row_block0=N_CTX // DEC_SEQ, s0=state_hgrn, layer=l)
        b_sgu_full = jnp.repeat(b_sgu[l].T, HEAD_DIM, axis=1)
        x = _mix(a_ctx, a_lat, z, x, mod, row(sgu_norm), w_sgu, b_sgu_full, w_conv_c[l], w_pool,
                 row(pool_scale), w_out_l, l)
        if l + 1 < DEPTH:
            x, h = _ffn(x, row(norm_ffn), mod, w_up_l, w_conv_ffn, w_down_l, l,
                        next_gain=norm_mix[l + 1].reshape(1, -1))
        else:
            x, = _ffn(x, row(norm_ffn), mod, w_up_l, w_conv_ffn, w_down_l, l)
        states.append(st)

    gain = norm_final.reshape(1, -1)
    y_prompt = _final_norm(x, gain, 0, N_TM_CTX).reshape(BATCH, SEQ, D_MODEL)
    y_sample = _final_norm(x, gain, N_TM_CTX, N_TM - N_TM_CTX).reshape(DEC_BATCH, DEC_SEQ, D_MODEL)
    return (y_prompt, y_sample, jnp.stack(states, axis=1))
```

```python
import functools

import numpy as np
import jax
import jax.numpy as jnp
from jax import lax
from jax.experimental import pallas as pl
from jax.experimental.pallas import tpu as pltpu

F32 = jnp.float32
BF16 = jnp.bfloat16

D_MODEL = 2048
BATCH, SEQ = 16, 256
DEC_BATCH, DEC_SEQ = 2, 1024
DEPTH = 2
GRID_W = 64
HEAD_DIM = 128
W_GROUP = D_MODEL // 4
N_HEADS = W_GROUP // HEAD_DIM
POOL_WINDOWS = (2, 4, 8, 16)
POOL_HALO = 8
CHUNK_B = 128
D_FF = 5632
D_IN = 11 * W_GROUP
EPS = 1e-6

N_CTX = BATCH * SEQ
N_LAT = DEC_BATCH * DEC_SEQ
N_TOK = N_CTX + N_LAT
MOD_ROWS = 8

TM = 1024
N_TM = N_TOK // TM
N_TM_CTX = N_CTX // TM
TM_IN = 2048
TN_IN = 512
TF = 512
FFN_CT = 256
FFN_RC = 64
FFN_NOUT = 512
TMIX = 256
N_TMIX_CTX = N_CTX // TMIX
TMIX_PER_LAT = DEC_SEQ // TMIX
TP = 512
N_TP_CTX = N_CTX // TP
TP_PER_LAT = DEC_SEQ // TP
TN_ADA = 2048
NORM_ROWS = 128
CAST_ROWS = 32

HC = 64
HC_LEVELS = 6
HC_FINE_LEVELS = 3
VMEM_LIMIT = 56 * 1024 * 1024


def _silu(x):
    return x * jax.nn.sigmoid(x)


def _dot(a, b):
    return jnp.dot(a, b, preferred_element_type=F32)


def _dot_nt(a, b):
    return lax.dot_general(a, b, (((1,), (1,)), ((), ())), preferred_element_type=F32)


def _params(*sem):
    return pltpu.CompilerParams(dimension_semantics=sem, vmem_limit_bytes=VMEM_LIMIT)


def _adaln_kernel(c_ref, w_ref, b_ref, o_ref):
    s = _silu(c_ref[...]).astype(BF16)
    o_ref[...] = _dot(s, w_ref[...].astype(BF16)) + b_ref[...]


def _adaln(cvec, w_ada, b_ada):
    n_out = w_ada.shape[-1]
    return pl.pallas_call(
        _adaln_kernel,
        grid=(DEPTH, n_out // TN_ADA),
        in_specs=[
            pl.BlockSpec((MOD_ROWS, D_MODEL), lambda l, j: (0, 0)),
            pl.BlockSpec((None, D_MODEL, TN_ADA), lambda l, j: (l, 0, j)),
            pl.BlockSpec((None, 1, TN_ADA), lambda l, j: (l, 0, j)),
        ],
        out_specs=pl.BlockSpec((None, MOD_ROWS, TN_ADA), lambda l, j: (l, 0, j)),
        out_shape=jax.ShapeDtypeStruct((DEPTH, MOD_ROWS, n_out), F32),
        compiler_params=_params("arbitrary", "arbitrary"),
        name="adaln",
    )(cvec, w_ada, b_ada.reshape(DEPTH, 1, n_out))


def _norm_mod_rows(x_ref, h_ref, gain, shift, scale, n_rows):
    gain_eff = gain * (1.0 + scale)

    def body(n, carry):
        r0 = pl.multiple_of(n * NORM_ROWS, NORM_ROWS)
        x = x_ref[pl.ds(r0, NORM_ROWS), :]
        ms = jnp.mean(x * x, axis=-1, keepdims=True)
        h_ref[pl.ds(r0, NORM_ROWS), :] = (x * lax.rsqrt(ms + EPS) * gain_eff + shift).astype(BF16)
        return carry
    lax.fori_loop(0, n_rows // NORM_ROWS, body, 0)


def _mod_row_of_big_tile(i):
    return jnp.maximum(i - (N_TM_CTX - 1), 0)


def _mod_spec(layer, k):
    return pl.BlockSpec((None, MOD_ROWS, D_MODEL), lambda *_: (layer, 0, k))


def _prep_kernel(xp_ref, xs_ref, pos_ref, g_ref, sh_ref, sc_ref, x_ref, h_ref):
    i = pl.program_id(0)
    lat_tile = jnp.maximum(i - N_TP_CTX, 0)
    r = jnp.where(i < N_TP_CTX, 0, 1 + lat_tile // TP_PER_LAT)

    @pl.when(i < N_TP_CTX)
    def _():
        def body(n, carry):
            rows = pl.ds(pl.multiple_of(n * NORM_ROWS, NORM_ROWS), NORM_ROWS)
            x_ref[rows, :] = xp_ref[rows, :]
            return carry
        lax.fori_loop(0, TP // NORM_ROWS, body, 0)

    @pl.when(i >= N_TP_CTX)
    def _():
        def body(n, carry):
            rows = pl.ds(pl.multiple_of(n * NORM_ROWS, NORM_ROWS), NORM_ROWS)
            x_ref[rows, :] = xs_ref[rows, :] + pos_ref[rows, :]
            return carry
        lax.fori_loop(0, TP // NORM_ROWS, body, 0)

    _norm_mod_rows(x_ref, h_ref, g_ref[...], sh_ref[pl.ds(r, 1), :], sc_ref[pl.ds(r, 1), :], TP)


def _prep(x_prompt, x_sample, pos, gain, mod, layer):
    lat = lambda i: jnp.maximum(i - N_TP_CTX, 0)
    return pl.pallas_call(
        _prep_kernel,
        grid=(N_TOK // TP,),
        in_specs=[
            pl.BlockSpec((TP, D_MODEL), lambda i: (jnp.minimum(i, N_TP_CTX - 1), 0)),
            pl.BlockSpec((TP, D_MODEL), lambda i: (lat(i), 0)),
            pl.BlockSpec((TP, D_MODEL), lambda i: (lat(i) % TP_PER_LAT, 0)),
            pl.BlockSpec((1, D_MODEL), lambda i: (0, 0)),
            _mod_spec(layer, 0),
            _mod_spec(layer, 1),
        ],
        out_specs=[pl.BlockSpec((TP, D_MODEL), lambda i: (i, 0)),
                   pl.BlockSpec((TP, D_MODEL), lambda i: (i, 0))],
        out_shape=[jax.ShapeDtypeStruct((N_TOK, D_MODEL), F32),
                   jax.ShapeDtypeStruct((N_TOK, D_MODEL), BF16)],
        compiler_params=_params("arbitrary"),
        name="prep",
    )(x_prompt, x_sample, pos, gain, mod, mod)


def _inproj_kernel(h_ref, w_ref, z_ref):
    z_ref[...] = _dot(h_ref[...], w_ref[...].astype(BF16))


def _inproj(h, w_in, layer):
    return pl.pallas_call(
        _inproj_kernel,
        grid=(N_TOK // TM_IN, D_IN // TN_IN),
        in_specs=[
            pl.BlockSpec((TM_IN, D_MODEL), lambda i, j: (i, 0)),
            pl.BlockSpec((None, D_MODEL, TN_IN), lambda i, j: (layer, 0, j)),
        ],
        out_specs=pl.BlockSpec((TM_IN, TN_IN), lambda i, j: (i, j)),
        out_shape=jax.ShapeDtypeStruct((N_TOK, D_IN), F32),
        compiler_params=_params("arbitrary", "arbitrary"),
        name="inproj",
    )(h, w_in)


def _hgrn_constants():
    t = np.arange(HC)
    run = [(t[None, :] <= t[:, None]), (t[None, :] >= t[:, None])]
    g = np.zeros((2, HC_FINE_LEVELS + 1, HC, HC), np.float32)
    uq = np.ones((2, HC_LEVELS + 1, HC), np.float32)
    bm = np.zeros((HC_LEVELS + 1, HC, HC), np.float32)
    for d in range(2):
        g[d, 0] = run[d]
        for lev in range(HC_LEVELS):
            m = 1 << lev
            base = (t // (2 * m)) * (2 * m)
            if lev < HC_FINE_LEVELS:
                g[d, lev + 1] = run[d][base + m - 1 + d]
            upper = (t % (2 * m)) >= m
            uq[d, lev] = upper if d == 0 else ~upper
    uk = 1.0 - uq
    uk[:, HC_LEVELS] = 1.0
    for lev in range(HC_LEVELS):
        m = 1 << lev
        bm[lev] = (t[:, None] // (2 * m)) == (t[None, :] // (2 * m))
    bm[HC_LEVELS] = np.eye(HC)
    pair = bm[None] * uq[:, :, :, None] * uk[:, :, None, :]
    g = g.reshape(2, (HC_FINE_LEVELS + 1) * HC, HC)
    g3 = np.concatenate([g, g, g], axis=-1)
    lanes = lambda a: np.ascontiguousarray(np.broadcast_to(a[..., None], a.shape + (HEAD_DIM,)))
    return (jnp.asarray(g3, BF16), jnp.asarray(lanes(uq[:, :HC_FINE_LEVELS]), F32),
            jnp.asarray(pair, F32))


def _hgrn_chunk(d, r0, z_ref, lb_ref, g3_ref, uq_ref, pair_ref, s_scr, o_scr):
    rows = pl.ds(r0, HC)
    lb = lb_ref[d:d + 1, :]
    f = lb + (1.0 - lb) * jax.nn.sigmoid(z_ref[rows, (2 + d) * W_GROUP:(3 + d) * W_GROUP])
    kk_all = 1.0 - f
    lf = jnp.log(f)

    l1 = lf.astype(BF16)
    r1 = lf - l1.astype(F32)
    l2 = r1.astype(BF16)
    l3 = (r1 - l2.astype(F32)).astype(BF16)
    sums = _dot(g3_ref[d], jnp.concatenate([l1, l2, l3], axis=0))

    for h in range(N_HEADS):
        cs = slice(h * HEAD_DIM, (h + 1) * HEAD_DIM)
        q = z_ref[rows, cs]
        vb = z_ref[rows, W_GROUP + h * HEAD_DIM:W_GROUP + (h + 1) * HEAD_DIM].astype(BF16)
        kk = kk_all[:, cs]
        cum = sums[0:HC, cs]

        lo, hi = (kk, q) if d == 0 else (q, kk)
        att = _dot_nt(q.astype(BF16), kk.astype(BF16)) * pair_ref[d, HC_LEVELS]
        for lev in range(HC_LEVELS):
            m = 1 << lev
            if lev < HC_FINE_LEVELS:
                bound = sums[(lev + 1) * HC:(lev + 2) * HC, cs]
                rows_qk = jnp.where(uq_ref[d, lev] > 0.5, q, kk)
            else:
                bound = jnp.concatenate(
                    [jnp.broadcast_to(cum[b0 + m - 1 + d:b0 + m + d, :], (2 * m, HEAD_DIM))
                     for b0 in range(0, HC, 2 * m)], axis=0)
                rows_qk = jnp.concatenate(
                    [part for b0 in range(0, HC, 2 * m)
                     for part in (lo[b0:b0 + m], hi[b0 + m:b0 + 2 * m])], axis=0)
            decayed = (rows_qk * jnp.exp(-jnp.abs(cum - bound))).astype(BF16)
            att = att + _dot_nt(decayed, decayed) * pair_ref[d, lev]

        far = cum[HC - 1:HC, :] if d == 0 else cum[0:1, :]
        s_old = s_scr[d * N_HEADS + h]
        q_in = (q * jnp.exp(cum)).astype(BF16)
        o_scr[rows, cs] = _dot(jnp.concatenate([q_in, att.astype(BF16)], axis=1),
                               jnp.concatenate([s_old.astype(BF16), vb], axis=0))

        k_out = kk * jnp.exp(-jnp.abs(cum - far))
        k_ext = jnp.concatenate([k_out, jnp.broadcast_to(jnp.exp(far), (8, HEAD_DIM))], axis=0)
        k_ext_t = k_ext.T
        s_scr[d * N_HEADS + h] = (k_ext_t[:, HC:HC + 1] * s_old
                                  + _dot(k_ext_t[:, 0:HC].astype(BF16), vb))


def _hgrn_kernel(*refs, n_chunks, has_s0, n_cast):
    z_ref, lb_ref, ng_ref, g3_ref, uq_ref, pair_ref = refs[:6]
    k = 6
    s0_ref = st_ref = None
    if has_s0:
        s0_ref = refs[k]
        k += 1
    cast_src = refs[k:k + n_cast]
    k += n_cast
    a_ref = refs[k]
    k += 1
    if not has_s0:
        st_ref = refs[k]
        k += 1
    cast_dst = refs[k:k + n_cast]
    k += n_cast
    s_scr, of_scr, ob_scr = refs[k:k + 3]

    for src, dst in zip(cast_src, cast_dst):
        for r0 in range(0, src.shape[0], CAST_ROWS):
            dst[r0:r0 + CAST_ROWS, :] = src[r0:r0 + CAST_ROWS, :].astype(BF16)

    for d in range(2):
        for h in range(N_HEADS):
            s_scr[d * N_HEADS + h] = (s0_ref[d, h] if has_s0
                                      else jnp.zeros((HEAD_DIM, HEAD_DIM), F32))

    def scan_body(n, carry):
        for d, o_scr in ((0, of_scr), (1, ob_scr)):
            r0 = pl.multiple_of((n if d == 0 else n_chunks - 1 - n) * HC, HC)
            _hgrn_chunk(d, r0, z_ref, lb_ref, g3_ref, uq_ref, pair_ref, s_scr, o_scr)
        return carry
    lax.fori_loop(0, n_chunks, scan_body, 0, unroll=4)

    def out_body(n, carry):
        rows = pl.ds(pl.multiple_of(n * HC, HC), HC)
        o = of_scr[rows, :] + ob_scr[rows, :]
        parts = []
        for h in range(N_HEADS):
            oh = o[:, h * HEAD_DIM:(h + 1) * HEAD_DIM]
            parts.append(oh * lax.rsqrt(jnp.mean(oh * oh, axis=-1, keepdims=True) + EPS))
        gate = z_ref[rows, 4 * W_GROUP:5 * W_GROUP]
        a_ref[rows, :] = jnp.concatenate(parts, axis=-1) * ng_ref[...] * _silu(gate)
        return carry
    lax.fori_loop(0, n_chunks, out_body, 0)

    if st_ref is not None:
        for d in range(2):
            for h in range(N_HEADS):
                st_ref[d, h] = s_scr[d * N_HEADS + h]


def _hgrn(z, lb, norm_g, consts, *, seq_len, n_seq, row_block0, s0=None, layer=0, casts=()):
    has_s0 = s0 is not None
    const_specs = [pl.BlockSpec(c.shape, lambda b, nd=c.ndim: (0,) * nd) for c in consts]
    in_specs = [
        pl.BlockSpec((seq_len, 5 * W_GROUP), lambda b: (row_block0 + b, 0)),
        pl.BlockSpec((2, W_GROUP), lambda b: (0, 0)),
        pl.BlockSpec((1, W_GROUP), lambda b: (0, 0)),
    ] + const_specs
    args = [z, lb, norm_g, *consts]
    if has_s0:
        in_specs.append(pl.BlockSpec((None, None, 2, N_HEADS, HEAD_DIM, HEAD_DIM),
                                     lambda b: (b, layer, 0, 0, 0, 0)))
        args.append(s0)
    out_specs = [pl.BlockSpec((seq_len, W_GROUP), lambda b: (b, 0))]
    out_shape = [jax.ShapeDtypeStruct((n_seq * seq_len, W_GROUP), F32)]
    if not has_s0:
        out_specs.append(pl.BlockSpec((None, 2, N_HEADS, HEAD_DIM, HEAD_DIM),
                                      lambda b: (b, 0, 0, 0, 0)))
        out_shape.append(jax.ShapeDtypeStruct((n_seq, 2, N_HEADS, HEAD_DIM, HEAD_DIM), F32))
    for w in casts:
        slab, cols = w.shape[1] // n_seq, w.shape[2]
        assert slab * n_seq == w.shape[1] and slab % CAST_ROWS == 0
        in_specs.append(pl.BlockSpec((None, slab, cols), lambda b: (layer, b, 0)))
        args.append(w)
        out_specs.append(pl.BlockSpec((slab, cols), lambda b: (b, 0)))
        out_shape.append(jax.ShapeDtypeStruct(w.shape[1:], BF16))
    kern = functools.partial(_hgrn_kernel, n_chunks=seq_len // HC, has_s0=has_s0,
                             n_cast=len(casts))
    return pl.pallas_call(
        kern,
        grid=(n_seq,),
        in_specs=in_specs,
        out_specs=out_specs,
        out_shape=out_shape,
        scratch_shapes=[
            pltpu.VMEM((2 * N_HEADS, HEAD_DIM, HEAD_DIM), F32),
            pltpu.VMEM((seq_len, W_GROUP), F32),
            pltpu.VMEM((seq_len, W_GROUP), F32),
        ],
        compiler_params=_params("arbitrary"),
        name="hgrn_lat" if has_s0 else "hgrn_ctx",
    )(*args)


def _mix_kernel(actx_ref, alat_ref, bu_ref, bv_ref, cb_ref, cc_ref, ch_ref, dx_ref,
                zp_ref, zn_ref,
                x_ref, g1_ref, sgun_ref, wsgu_ref, bsgu_ref, wconv_ref, wpool_ref, pscale_ref,
                wout_ref, o_ref, cat_scr):
    i = pl.program_id(0)
    is_ctx = i < N_TMIX_CTX
    lat_tile = jnp.maximum(i - N_TMIX_CTX, 0)
    j = jnp.where(is_ctx, 0, lat_tile % TMIX_PER_LAT)
    tiles_in_seq = jnp.where(is_ctx, 1, TMIX_PER_LAT)
    keep_prev = jnp.where(j == 0, 0.0, 1.0)
    keep_next = jnp.where(j == tiles_in_seq - 1, 0.0, 1.0)
    mod_row = jnp.where(is_ctx, 0, 1 + lat_tile // TMIX_PER_LAT)
    row = lax.broadcasted_iota(jnp.int32, (TMIX, 1), 0)

    def project(k):
        ks = slice(k * W_GROUP, (k + 1) * W_GROUP)
        return _dot(cat_scr[:, ks], wout_ref[ks, :])

    cat_scr[:, 0:W_GROUP] = jnp.where(is_ctx, actx_ref[...], alat_ref[...]).astype(BF16)
    o_ref[...] = project(0)

    u = jax.nn.gelu(bu_ref[...])
    vv = jax.nn.gelu(bv_ref[...])
    vn = (vv * lax.rsqrt(jnp.mean(vv * vv, axis=-1, keepdims=True) + EPS) * sgun_ref[...]).astype(BF16)
    for n in range(TMIX // CHUNK_B):
        rs = slice(n * CHUNK_B, (n + 1) * CHUNK_B)
        for h in range(N_HEADS):
            cs = slice(h * HEAD_DIM, (h + 1) * HEAD_DIM)
            mixed = _dot(wsgu_ref[h], vn[rs, cs]) + bsgu_ref[:, cs]
            cat_scr[rs, W_GROUP + h * HEAD_DIM:W_GROUP + (h + 1) * HEAD_DIM] = (u[rs, cs] * mixed).astype(BF16)
    o_ref[...] += project(1)

    p = cc_ref[...] * ch_ref[...]
    cc_cols, ch_cols, dx_cols = (slice(k * W_GROUP, (k + 1) * W_GROUP) for k in (8, 9, 10))
    last = slice(POOL_HALO - 1, POOL_HALO)
    p_edge_prev = zp_ref[last, cc_cols] * zp_ref[last, ch_cols] * keep_prev
    p_edge_next = zn_ref[0:1, cc_cols] * zn_ref[0:1, ch_cols] * keep_next
    p_prev = jnp.where(row == 0, p_edge_prev, pltpu.roll(p, 1, 0))
    p_next = jnp.where(row == TMIX - 1, p_edge_next, pltpu.roll(p, TMIX - 1, 0))
    wc = wconv_ref[...]
    c_out = cb_ref[...] * (wc[0:1] * p_prev + wc[1:2] * p + wc[2:3] * p_next)
    cat_scr[:, 2 * W_GROUP:3 * W_GROUP] = c_out.astype(BF16)
    o_ref[...] += project(2)

    xd = dx_ref[...]
    ext = jnp.concatenate([zp_ref[:, dx_cols] * keep_prev, xd, zn_ref[:, dx_cols] * keep_next], axis=0)
    n_ext = TMIX + 2 * POOL_HALO
    back = lambda a, s: pltpu.roll(a, s, 0)
    ahead = lambda a, s: pltpu.roll(a, n_ext - s, 0)
    gd = W_GROUP // len(POOL_WINDOWS)
    sums = [back(ext, 1) + ext]
    for g, s in enumerate((1, 2, 4)):
        wider = sums[-1][:, gd:]
        sums.append(back(wider, s) + ahead(wider, s))
    pos = j * TMIX + row
    seq_len = tiles_in_seq * TMIX
    for g, win in enumerate(POOL_WINDOWS):
        cs = slice(g * gd, (g + 1) * gd)
        cnt = (jnp.minimum(pos + win // 2, seq_len) - jnp.maximum(pos - win // 2, 0)).astype(F32)
        mean = sums[g][POOL_HALO:POOL_HALO + TMIX, 0:gd] / cnt
        dg = _dot((mean - xd[:, cs]).astype(BF16), wpool_ref[g]) * pscale_ref[:, cs]
        cat_scr[:, 3 * W_GROUP + g * gd:3 * W_GROUP + (g + 1) * gd] = dg.astype(BF16)

    y = o_ref[...] + project(3)
    o_ref[...] = x_ref[...] + g1_ref[pl.ds(mod_row, 1), :] * y


def _mix(a_ctx, a_lat, z, x, mod, sgu_norm, w_sgu, b_sgu_full, w_conv, w_pool, pool_scale, w_out,
         layer):
    halo_blocks = TMIX // POOL_HALO
    last_halo = N_TOK // POOL_HALO - 1
    zcol = lambda k: pl.BlockSpec((TMIX, W_GROUP), lambda i: (i, k))
    zprev = pl.BlockSpec((POOL_HALO, D_IN), lambda i: (jnp.maximum(i * halo_blocks - 1, 0), 0))
    znext = pl.BlockSpec((POOL_HALO, D_IN),
                         lambda i: (jnp.minimum((i + 1) * halo_blocks, last_halo), 0))
    full = lambda arr: pl.BlockSpec(arr.shape, lambda i, nd=arr.ndim: (0,) * nd)
    of_layer = lambda arr: pl.BlockSpec((None,) + arr.shape[1:],
                                        lambda i, nd=arr.ndim: (layer,) + (0,) * (nd - 1))
    return pl.pallas_call(
        _mix_kernel,
        grid=(N_TOK // TMIX,),
        in_specs=[
            pl.BlockSpec((TMIX, W_GROUP), lambda i: (jnp.minimum(i, N_TMIX_CTX - 1), 0)),
            pl.BlockSpec((TMIX, W_GROUP), lambda i: (jnp.maximum(i - N_TMIX_CTX, 0), 0)),
            zcol(5), zcol(6), zcol(7), zcol(8), zcol(9), zcol(10),
            zprev, znext,
            pl.BlockSpec((TMIX, D_MODEL), lambda i: (i, 0)),
            _mod_spec(layer, 2),
            full(sgu_norm), of_layer(w_sgu), full(b_sgu_full), full(w_conv), of_layer(w_pool),
            full(pool_scale), full(w_out),
        ],
        out_specs=pl.BlockSpec((TMIX, D_MODEL), lambda i: (i, 0)),
        out_shape=jax.ShapeDtypeStruct((N_TOK, D_MODEL), F32),
        scratch_shapes=[pltpu.VMEM((TMIX, D_MODEL), BF16)],
        compiler_params=_params("arbitrary"),
        name="mix",
    )(a_ctx, a_lat, z, z, z, z, z, z, z, z, x, mod, sgu_norm, w_sgu, b_sgu_full, w_conv, w_pool,
      pool_scale, w_out)


def _ffn_kernel(x_ref, g_ref, sh_ref, sc_ref, g2_ref, wv_ref, wg_ref, cv_ref, cg_ref, wd_ref,
                *rest, emit_next):
    if emit_next:
        gn_ref, shn_ref, scn_ref, o_ref, hn_ref, uv_scr, ug_scr, act_scr = rest
        h_scr = hn_ref
    else:
        o_ref, h_scr, uv_scr, ug_scr, act_scr = rest
    i = pl.program_id(0)
    j = pl.program_id(1)
    r = _mod_row_of_big_tile(i)

    @pl.when(j == 0)
    def _():
        _norm_mod_rows(x_ref, h_scr, g_ref[...], sh_ref[pl.ds(r, 1), :], sc_ref[pl.ds(r, 1), :], TM)

        def zero_body(n, carry):
            rows = pl.ds(pl.multiple_of(n * NORM_ROWS, NORM_ROWS), NORM_ROWS)
            o_ref[rows, :] = jnp.zeros((NORM_ROWS, D_MODEL), F32)
            return carry
        lax.fori_loop(0, TM // NORM_ROWS, zero_body, 0)

    inner_keep = jnp.where(i < N_TM_CTX, 0.0, 1.0)

    def conv3_rows(u_scr, w, r0, cs):
        cur = u_scr[r0:r0 + FFN_RC, cs]
        if r0 == 0:
            prev = jnp.zeros((POOL_HALO, FFN_CT), F32)
        else:
            prev = u_scr[r0 - POOL_HALO:r0, cs]
            if r0 % SEQ == 0:
                prev = prev * inner_keep
        r1 = r0 + FFN_RC
        if r1 == TM:
            nxt = jnp.zeros((POOL_HALO, FFN_CT), F32)
        else:
            nxt = u_scr[r1:r1 + POOL_HALO, cs]
            if r1 % SEQ == 0:
                nxt = nxt * inner_keep
        ext = jnp.concatenate([prev, cur, nxt], axis=0)
        n_ext = FFN_RC + 2 * POOL_HALO
        before = pltpu.roll(ext, 1, 0)[POOL_HALO:POOL_HALO + FFN_RC]
        after = pltpu.roll(ext, n_ext - 1, 0)[POOL_HALO:POOL_HALO + FFN_RC]
        return (w[0:1] * before.astype(BF16) + w[1:2] * cur.astype(BF16)
                + w[2:3] * after.astype(BF16))

    col_tiles = [slice(c * FFN_CT, (c + 1) * FFN_CT) for c in range(TF // FFN_CT)]
    for cs in col_tiles:
        uv_scr[:, cs] = _dot(h_scr[...], wv_ref[:, cs])
        ug_scr[:, cs] = _dot(h_scr[...], wg_ref[:, cs])
    for cs in col_tiles:
        cv = cv_ref[:, cs].astype(BF16)
        cg = cg_ref[:, cs].astype(BF16)
        for r0 in range(0, TM, FFN_RC):
            val = conv3_rows(uv_scr, cv, r0, cs)
            gate = conv3_rows(ug_scr, cg, r0, cs)
            act_scr[r0:r0 + FFN_RC, cs] = val * _silu(gate)
    for c in range(D_MODEL // FFN_NOUT):
        ns = slice(c * FFN_NOUT, (c + 1) * FFN_NOUT)
        o_ref[:, ns] += _dot(act_scr[...], wd_ref[:, ns])

    @pl.when(j == pl.num_programs(1) - 1)
    def _():
        g2 = g2_ref[pl.ds(r, 1), :]

        def body(n, carry):
            rows = pl.ds(pl.multiple_of(n * NORM_ROWS, NORM_ROWS), NORM_ROWS)
            o_ref[rows, :] = x_ref[rows, :] + g2 * o_ref[rows, :]
            return carry
        lax.fori_loop(0, TM // NORM_ROWS, body, 0)
        if emit_next:
            _norm_mod_rows(o_ref, hn_ref, gn_ref[...], shn_ref[pl.ds(r, 1), :],
                           scn_ref[pl.ds(r, 1), :], TM)


def _ffn(x, gain, mod, w_up, w_conv, w_down, layer, next_gain=None):
    n_f = D_FF // TF
    emit_next = next_gain is not None
    tile = pl.BlockSpec((TM, D_MODEL), lambda i, j: (i, 0))
    next_specs, next_args = [], []
    out_specs = [tile]
    out_shape = [jax.ShapeDtypeStruct((N_TOK, D_MODEL), F32)]
    if emit_next:
        next_specs = [pl.BlockSpec((1, D_MODEL), lambda i, j: (0, 0)),
                      _mod_spec(layer + 1, 0), _mod_spec(layer + 1, 1)]
        next_args = [next_gain, mod, mod]
        out_specs.append(tile)
        out_shape.append(jax.ShapeDtypeStruct((N_TOK, D_MODEL), BF16))
    return pl.pallas_call(
        functools.partial(_ffn_kernel, emit_next=emit_next),
        grid=(N_TM, n_f),
        in_specs=[
            pl.BlockSpec((TM, D_MODEL), lambda i, j: (i, 0)),
            pl.BlockSpec((1, D_MODEL), lambda i, j: (0, 0)),
            _mod_spec(layer, 3),
            _mod_spec(layer, 4),
            _mod_spec(layer, 5),
            pl.BlockSpec((D_MODEL, TF), lambda i, j: (0, j)),
            pl.BlockSpec((D_MODEL, TF), lambda i, j: (0, j + n_f)),
            pl.BlockSpec((None, 3, TF), lambda i, j: (layer, 0, j)),
            pl.BlockSpec((None, 3, TF), lambda i, j: (layer, 0, j + n_f)),
            pl.BlockSpec((TF, D_MODEL), lambda i, j: (j, 0)),
        ] + next_specs,
        out_specs=out_specs,
        out_shape=out_shape,
        scratch_shapes=([] if emit_next else [pltpu.VMEM((TM, D_MODEL), BF16)])
        + [pltpu.VMEM((TM, TF), F32), pltpu.VMEM((TM, TF), F32), pltpu.VMEM((TM, TF), BF16)],
        compiler_params=_params("arbitrary", "arbitrary"),
        name="ffn",
    )(x, gain, mod, mod, mod, w_up, w_up, w_conv, w_conv, w_down, *next_args)


def _final_norm_kernel(x_ref, g_ref, o_ref):
    gain = g_ref[...]

    def body(n, carry):
        rows = pl.ds(pl.multiple_of(n * NORM_ROWS, NORM_ROWS), NORM_ROWS)
        x = x_ref[rows, :]
        o_ref[rows, :] = x * lax.rsqrt(jnp.mean(x * x, axis=-1, keepdims=True) + EPS) * gain
        return carry
    lax.fori_loop(0, TM // NORM_ROWS, body, 0)


def _final_norm(x, gain, tile0, n_tiles):
    return pl.pallas_call(
        _final_norm_kernel,
        grid=(n_tiles,),
        in_specs=[pl.BlockSpec((TM, D_MODEL), lambda i: (tile0 + i, 0)),
                  pl.BlockSpec((1, D_MODEL), lambda i: (0, 0))],
        out_specs=pl.BlockSpec((TM, D_MODEL), lambda i: (i, 0)),
        out_shape=jax.ShapeDtypeStruct((n_tiles * TM, D_MODEL), F32),
        compiler_params=_params("arbitrary"),
        name="final_norm",
    )(x, gain)


def _grid_pos_embed():
    rows = DEC_SEQ // GRID_W
    quarter = D_MODEL // 4
    freq = jnp.exp(-jnp.log(10000.0) * jnp.arange(quarter, dtype=F32) / quarter)[None, :]
    r = jnp.arange(rows, dtype=F32)[:, None] * freq
    col = jnp.arange(GRID_W, dtype=F32)[:, None] * freq
    rep = lambda a: jnp.repeat(a, GRID_W, axis=0)
    til = lambda a: jnp.tile(a, (rows, 1))
    return jnp.concatenate([rep(jnp.sin(r)), rep(jnp.cos(r)), til(jnp.sin(col)), til(jnp.cos(col))], -1)


def kernel(x_prompt, x_sample, c, state_hgrn, c_ctx, w_ada, b_ada, norm_mix, norm_ffn, w_in,
           lb_logits, hgrn_norm, sgu_norm, w_sgu, b_sgu, w_conv_c, w_pool, pool_scale, w_out,
           w_up, w_conv_ffn, w_down, norm_final):
    p = jax.nn.softmax(lb_logits.astype(F32), axis=0)
    lbs = jnp.cumsum(p, axis=0) - p[0:1]

    cvec = jnp.concatenate([c_ctx[None, :], c, jnp.zeros((MOD_ROWS - 1 - DEC_BATCH, D_MODEL), F32)], 0)
    mod = _adaln(cvec, w_ada, b_ada)

    consts = _hgrn_constants()
    w_sgu, w_pool = w_sgu.astype(BF16), w_pool.astype(BF16)
    states = []
    x = None
    for l in range(DEPTH):
        row = lambda a: a[l].reshape(1, -1)
        if l == 0:
            x, h = _prep(x_prompt.reshape(N_CTX, D_MODEL), x_sample.reshape(N_LAT, D_MODEL),
                         _grid_pos_embed(), row(norm_mix), mod, l)
        z = _inproj(h, w_in, l)
        a_ctx, st, w_up_l, w_down_l, w_out_l = _hgrn(
            z, lbs[l], row(hgrn_norm), consts, seq_len=SEQ, n_seq=BATCH, row_block0=0, layer=l,
            casts=(w_up, w_down, w_out))
        a_lat, = _hgrn(z, lbs[l], row(hgrn_norm), consts, seq_len=DEC_SEQ, n_seq=DEC_BATCH,
                       row_block0=N_CTX // DEC_SEQ, s0=state_hgrn, layer=l)
        b_sgu_full = jnp.repeat(b_sgu[l].T, HEAD_DIM, axis=1)
        x = _mix(a_ctx, a_lat, z, x, mod, row(sgu_norm), w_sgu, b_sgu_full, w_conv_c[l], w_pool,
                 row(pool_scale), w_out_l, l)
        if l + 1 < DEPTH:
            x, h = _ffn(x, row(norm_ffn), mod, w_up_l, w_conv_ffn, w_down_l, l,
                        next_gain=norm_mix[l + 1].reshape(1, -1))
        else:
            x, = _ffn(x, row(norm_ffn), mod, w_up_l, w_conv_ffn, w_down_l, l)
        states.append(st)

    gain = norm_final.reshape(1, -1)
    y_prompt = _final_norm(x, gain, 0, N_TM_CTX).reshape(BATCH, SEQ, D_MODEL)
    y_sample = _final_norm(x, gain, N_TM_CTX, N_TM - N_TM_CTX).reshape(DEC_BATCH, DEC_SEQ, D_MODEL)
    return (y_prompt, y_sample, jnp.stack(states, axis=1))
```
